```python
import math
import jax, jax.numpy as jnp
from jax import lax
import numpy as np

D_MODEL = 1024
BATCH = 8
SEQ = 4096
DEPTH = 1

D_MIX = D_MODEL
D_RWKV = D_MIX // 2
D_SSM = D_MIX - D_RWKV
HEAD_DIM = 64
N_RWKV_HEADS = D_RWKV // HEAD_DIM
DECAY_LORA = 64
AAA_LORA = 64
GATE_LORA = 128
N_SHIFT = 3 * D_RWKV + DECAY_LORA + AAA_LORA + GATE_LORA
D_IN = N_SHIFT + D_SSM
SSM_GROUP = 16
N_SSM_GROUPS = D_SSM // SSM_GROUP
SSM_STATE = 64
DT_MIN = 1e-3
DT_MAX = 1e-1
N_EXPERTS = 32
TOP_K = 4
D_FF = D_MODEL
SWIGLU_ALPHA = 1.702
SWIGLU_LIMIT = 7.0
NORM_EPS = 1e-5
LN_X_EPS = 64e-5

kernel_name = "hymba_rwkv7_s5_moe_block"


def rmsnorm(x, g):
    x32 = x.astype(jnp.float32)
    y = x32 * lax.rsqrt(jnp.mean(x32 * x32, axis=-1, keepdims=True) + NORM_EPS)
    return (y * g.astype(jnp.float32)).astype(x.dtype)


def token_shift(p):
    return jnp.pad(p[:, :-1], ((0, 0), (1, 0), (0, 0)))


def rwkv7_recurrence(r, w, k, v, kk, b):
    bsz, _, h, n = r.shape
    xs = tuple(jnp.moveaxis(t.astype(jnp.float32), 1, 0) for t in (r, w, k, v, kk, b))

    def step(S, inp):
        r_t, w_t, k_t, v_t, kk_t, b_t = inp
        sa = jnp.einsum('bhvk,bhk->bhv', S, -kk_t)
        S = S * w_t[:, :, None, :] + sa[..., None] * b_t[:, :, None, :] + v_t[..., None] * k_t[:, :, None, :]
        y = jnp.einsum('bhvk,bhk->bhv', S, r_t)
        return S, y

    S0 = jnp.zeros((bsz, h, n, n), jnp.float32)
    _, y = lax.scan(step, S0, xs)
    return jnp.moveaxis(y, 0, 1)


def rwkv7_mix(p, w0, w_up, a0, a_up, g_up, k_k, k_a, r_k, ln_w, ln_b):
    bsz, t_len, _ = p.shape
    H, N = N_RWKV_HEADS, HEAD_DIM
    c1 = 3 * D_RWKV
    r, k, v, w_lr, a_lr, g_lr = jnp.split(
        p, [D_RWKV, 2 * D_RWKV, c1, c1 + DECAY_LORA, c1 + DECAY_LORA + AAA_LORA], axis=-1)
    log_w = -jax.nn.softplus(-(w0 + jnp.tanh(w_lr) @ w_up)) - 0.5
    decay = jnp.exp(-jnp.exp(log_w.astype(jnp.float32)))
    a = jax.nn.sigmoid(a0 + a_lr @ a_up)
    g = jax.nn.sigmoid(g_lr) @ g_up

    def heads(t):
        return t.reshape(bsz, t_len, H, N)

    kk = heads(k * k_k).astype(jnp.float32)
    kk = kk / jnp.maximum(jnp.sqrt(jnp.sum(kk * kk, axis=-1, keepdims=True)), 1e-12)
    k = k * (1.0 + (a - 1.0) * k_a)
    y = rwkv7_recurrence(heads(r), heads(decay), heads(k), heads(v), kk, kk * heads(a))
    mu = jnp.mean(y, axis=-1, keepdims=True)
    var = jnp.mean(jnp.square(y - mu), axis=-1, keepdims=True)
    y = ((y - mu) * lax.rsqrt(var + LN_X_EPS)).reshape(bsz, t_len, D_RWKV) * ln_w + ln_b
    bonus = jnp.sum(heads(r) * heads(k) * r_k, axis=-1, keepdims=True) * heads(v)
    y = (y + bonus.reshape(bsz, t_len, D_RWKV)) * g
    return y.astype(p.dtype)


def s5_mix(u, lambda_re, lambda_im, log_step, b_re, b_im, c_re, c_im, d_skip, w_glu, b_glu):
    bsz, t_len, _ = u.shape
    G, P, M = N_SSM_GROUPS, SSM_STATE, SSM_GROUP
    f32 = jnp.float32
    u32 = u.astype(f32)
    ug = u32.reshape(bsz, t_len, G, M)
    lam_re = jnp.minimum(lambda_re.astype(f32), -1e-4)
    lam_im = lambda_im.astype(f32)
    dt = jnp.exp(log_step.astype(f32))[:, None]
    mag = jnp.exp(lam_re * dt)
    lb_re = mag * jnp.cos(lam_im * dt)
    lb_im = mag * jnp.sin(lam_im * dt)
    den = lam_re * lam_re + lam_im * lam_im
    num_re = lb_re - 1.0
    z_re = (num_re * lam_re + lb_im * lam_im) / den
    z_im = (lb_im * lam_re - num_re * lam_im) / den
    b_re32, b_im32 = b_re.astype(f32), b_im.astype(f32)
    bb_re = z_re[..., None] * b_re32 - z_im[..., None] * b_im32
    bb_im = z_re[..., None] * b_im32 + z_im[..., None] * b_re32
    bu_re = jnp.einsum('btgm,gpm->tbgp', ug, bb_re)
    bu_im = jnp.einsum('btgm,gpm->tbgp', ug, bb_im)
    a_re = jnp.broadcast_to(lb_re[None, None], (t_len, 1, G, P))
    a_im = jnp.broadcast_to(lb_im[None, None], (t_len, 1, G, P))

    def combine(e1, e2):
        a1r, a1i, b1r, b1i = e1
        a2r, a2i, b2r, b2i = e2
        return (a1r * a2r - a1i * a2i,
                a1r * a2i + a1i * a2r,
                a2r * b1r - a2i * b1i + b2r,
                a2r * b1i + a2i * b1r + b2i)

    _, _, s_re, s_im = lax.associative_scan(combine, (a_re, a_im, bu_re, bu_im), axis=0)
    y = (jnp.einsum('tbgp,gmp->btgm', s_re, c_re.astype(f32))
         - jnp.einsum('tbgp,gmp->btgm', s_im, c_im.astype(f32)))
    y = y.reshape(bsz, t_len, D_SSM) + d_skip.astype(f32) * u32
    y = jax.nn.gelu(y)
    y = y * jax.nn.sigmoid(y @ w_glu.astype(f32) + b_glu.astype(f32))
    return y


def moe_ffn(h, router_w, router_b, w1, b1, w2, b2):
    bsz, t_len, d = h.shape
    ht = h.reshape(-1, d)
    logits = (ht @ router_w + router_b).astype(jnp.float32)
    top_val, top_idx = lax.top_k(logits, TOP_K)
    gates = jax.nn.softmax(top_val, axis=-1)
    comb = jnp.einsum('nk,nke->ne', gates, jax.nn.one_hot(top_idx, N_EXPERTS, dtype=jnp.float32))
    out = jnp.zeros((ht.shape[0], d), jnp.float32)
    for e in range(N_EXPERTS):
        hu = ht @ w1[e] + b1[e]
        x_glu = jnp.minimum(hu[:, 0::2], SWIGLU_LIMIT)
        x_lin = jnp.clip(hu[:, 1::2], -SWIGLU_LIMIT, SWIGLU_LIMIT)
        act = x_glu * jax.nn.sigmoid(SWIGLU_ALPHA * x_glu) * (x_lin + 1.0)
        out = out + comb[:, e:e + 1] * (act @ w2[e] + b2[e])
    return out.reshape(bsz, t_len, d).astype(h.dtype)


def setup_inputs(seed: int = 0) -> dict:
    key = jax.random.key(seed)
    ks = jax.random.split(key, 40)
    L = DEPTH
    f32 = jnp.float32

    def nrm(i, shape, scale):
        return scale * jax.random.normal(ks[i], shape, f32)

    def gain(i, shape):
        return 1.0 + nrm(i, shape, 0.02)

    x = nrm(0, (BATCH, SEQ, D_MODEL), 1.0)
    norm1_g = gain(1, (L, D_MODEL))
    w_in = nrm(2, (L, D_MODEL, D_IN), D_MODEL ** -0.5)
    mu_shift = jax.random.uniform(ks[3], (L, N_SHIFT), f32)
    ratio = jnp.arange(L, dtype=f32)[:, None] / max(DEPTH - 1, 1)
    frac = jnp.arange(D_RWKV, dtype=f32)[None, :] / (D_RWKV - 1)
    w0 = -7.0 + 5.0 * frac ** (0.85 + jnp.sqrt(ratio)) + 0.5 + nrm(4, (L, D_RWKV), 0.02)
    w_up = nrm(5, (L, DECAY_LORA, D_RWKV), 0.1 * DECAY_LORA ** -0.5)
    a0 = nrm(6, (L, D_RWKV), 0.1)
    a_up = nrm(7, (L, AAA_LORA, D_RWKV), 0.1 * AAA_LORA ** -0.5)
    g_up = nrm(8, (L, GATE_LORA, D_RWKV), GATE_LORA ** -0.5)
    k_k = 0.85 + nrm(9, (L, D_RWKV), 0.02)
    k_a = 1.0 + nrm(10, (L, D_RWKV), 0.02)
    r_k = -0.04 + nrm(11, (L, N_RWKV_HEADS, HEAD_DIM), 0.02)
    ln_x_w = gain(12, (L, D_RWKV))
    ln_x_b = nrm(13, (L, D_RWKV), 0.01)
    lambda_re = -0.5 + nrm(14, (L, N_SSM_GROUPS, SSM_STATE), 0.01)
    lambda_im = (jnp.pi * jnp.arange(SSM_STATE, dtype=f32))[None, None, :] + nrm(15, (L, N_SSM_GROUPS, SSM_STATE), 0.01)
    log_step = jax.random.uniform(ks[16], (L, N_SSM_GROUPS), f32, math.log(DT_MIN), math.log(DT_MAX))
    b_re = nrm(17, (L, N_SSM_GROUPS, SSM_STATE, SSM_GROUP), (2 * SSM_GROUP) ** -0.5)
    b_im = nrm(18, (L, N_SSM_GROUPS, SSM_STATE, SSM_GROUP), (2 * SSM_GROUP) ** -0.5)
    c_re = nrm(19, (L, N_SSM_GROUPS, SSM_GROUP, SSM_STATE), (2 * SSM_STATE) ** -0.5)
    c_im = nrm(20, (L, N_SSM_GROUPS, SSM_GROUP, SSM_STATE), (2 * SSM_STATE) ** -0.5)
    d_skip = nrm(21, (L, D_SSM), 1.0)
    w_glu = nrm(22, (L, D_SSM, D_SSM), D_SSM ** -0.5)
    b_glu = nrm(23, (L, D_SSM), 0.01)
    beta_ssm = gain(24, (L, D_SSM))
    w_out = nrm(25, (L, D_MIX, D_MODEL), D_MIX ** -0.5)
    norm2_g = gain(26, (L, D_MODEL))
    router_w = nrm(27, (L, D_MODEL, N_EXPERTS), D_MODEL ** -0.5)
    router_b = nrm(28, (L, N_EXPERTS), 0.01)
    w1 = nrm(29, (L, N_EXPERTS, D_MODEL, 2 * D_FF), D_MODEL ** -0.5)
    b1 = nrm(30, (L, N_EXPERTS, 2 * D_FF), 0.01)
    w2 = nrm(31, (L, N_EXPERTS, D_FF, D_MODEL), D_FF ** -0.5)
    b2 = nrm(32, (L, N_EXPERTS, D_MODEL), 0.01)
    final_g = gain(33, (D_MODEL,))
    return {"x": x, "norm1_g": norm1_g, "w_in": w_in, "mu_shift": mu_shift,
            "w0": w0, "w_up": w_up, "a0": a0, "a_up": a_up, "g_up": g_up,
            "k_k": k_k, "k_a": k_a, "r_k": r_k, "ln_x_w": ln_x_w, "ln_x_b": ln_x_b,
            "lambda_re": lambda_re, "lambda_im": lambda_im, "log_step": log_step,
            "b_re": b_re, "b_im": b_im, "c_re": c_re, "c_im": c_im, "d_skip": d_skip,
            "w_glu": w_glu, "b_glu": b_glu, "beta_ssm": beta_ssm, "w_out": w_out,
            "norm2_g": norm2_g, "router_w": router_w, "router_b": router_b,
            "w1": w1, "b1": b1, "w2": w2, "b2": b2, "final_g": final_g}


def reference(x, norm1_g, w_in, mu_shift, w0, w_up, a0, a_up, g_up, k_k, k_a, r_k,
              ln_x_w, ln_x_b, lambda_re, lambda_im, log_step, b_re, b_im, c_re, c_im,
              d_skip, w_glu, b_glu, beta_ssm, w_out, norm2_g, router_w, router_b,
              w1, b1, w2, b2, final_g):
    for l in range(DEPTH):
        h = rmsnorm(x, norm1_g[l])
        proj = h @ w_in[l]
        ps, u = proj[..., :N_SHIFT], proj[..., N_SHIFT:]
        ps = ps + mu_shift[l] * (token_shift(ps) - ps)
        y_rwkv = rwkv7_mix(ps, w0[l], w_up[l], a0[l], a_up[l], g_up[l], k_k[l], k_a[l], r_k[l],
                           ln_x_w[l], ln_x_b[l])
        y_ssm = rmsnorm(s5_mix(u, lambda_re[l], lambda_im[l], log_step[l], b_re[l], b_im[l],
                               c_re[l], c_im[l], d_skip[l], w_glu[l], b_glu[l]), beta_ssm[l])
        mixed = jnp.concatenate([y_rwkv.astype(x.dtype), y_ssm.astype(x.dtype)], axis=-1)
        x = x + mixed @ w_out[l]
        x = x + moe_ffn(rmsnorm(x, norm2_g[l]), router_w[l], router_b[l], w1[l], b1[l], w2[l], b2[l])
    return rmsnorm(x, final_g)
```

```python
import functools

import jax
import jax.numpy as jnp
from jax import lax
from jax.experimental import pallas as pl
from jax.experimental.pallas import tpu as pltpu

F32 = jnp.float32
BF16 = jnp.bfloat16
_MXU = jnp.bfloat16

D_MODEL = 1024
D_RWKV = 512
D_SSM = 512
HEAD = 64
N_HEADS = D_RWKV // HEAD
LORA_W = 64
LORA_A = 64
LORA_G = 128
N_SHIFT = 3 * D_RWKV + LORA_W + LORA_A + LORA_G
D_IN = N_SHIFT + D_SSM
SSM_GROUP = 16
N_GROUPS = D_SSM // SSM_GROUP
SSM_STATE = 64
N_STATE = N_GROUPS * SSM_STATE
N_EXPERTS = 32
TOP_K = 4
D_FF = D_MODEL
SWIGLU_ALPHA = 1.702
SWIGLU_LIMIT = 7.0
NORM_EPS = 1e-5
LN_X_EPS = 64e-5

LANES = 128
CHUNK = 64
PAIR = 2 * HEAD
N_PAIRS = D_RWKV // PAIR
FFN_TILE = 512
VMEM_LIMIT = 56 * 1024 * 1024


def _mm(a, b):
    return jnp.dot(a.astype(_MXU), b.astype(_MXU), preferred_element_type=F32)


def _mm_nt(a, b):
    return lax.dot_general(a.astype(_MXU), b.astype(_MXU), (((1,), (1,)), ((), ())),
                           preferred_element_type=F32)


def _mm_tn(a, b):
    return lax.dot_general(a.astype(_MXU), b.astype(_MXU), (((0,), (0,)), ((), ())),
                           preferred_element_type=F32)


def _split3(a):
    h1 = a.astype(_MXU)
    r = a - h1.astype(F32)
    h2 = r.astype(_MXU)
    r = r - h2.astype(F32)
    return h1, h2, r.astype(_MXU)


def _mm_f32_lhs(a, m):
    mb = m.astype(_MXU)
    h1, h2, h3 = _split3(a)
    out = jnp.dot(h1, mb, preferred_element_type=F32)
    out = out + jnp.dot(h2, mb, preferred_element_type=F32)
    return out + jnp.dot(h3, mb, preferred_element_type=F32)


def _mm_f32_rhs(m, b):
    mb = m.astype(_MXU)
    h1, h2, h3 = _split3(b)
    out = jnp.dot(mb, h1, preferred_element_type=F32)
    out = out + jnp.dot(mb, h2, preferred_element_type=F32)
    return out + jnp.dot(mb, h3, preferred_element_type=F32)


def _mm_hi(a, b):
    a1 = a.astype(_MXU)
    a2 = (a - a1.astype(F32)).astype(_MXU)
    b1 = b.astype(_MXU)
    b2 = (b - b1.astype(F32)).astype(_MXU)
    out = jnp.dot(a1, b1, preferred_element_type=F32)
    out = out + jnp.dot(a1, b2, preferred_element_type=F32)
    return out + jnp.dot(a2, b1, preferred_element_type=F32)


def _sigmoid(z):
    return 1.0 / (1.0 + jnp.exp(-z))


def _rms(x, g):
    ms = jnp.mean(x * x, axis=-1, keepdims=True)
    return x * lax.rsqrt(ms + NORM_EPS) * g


def _head_sum_matrix(scale):
    r = lax.broadcasted_iota(jnp.int32, (D_RWKV, D_RWKV), 0) // HEAD
    c = lax.broadcasted_iota(jnp.int32, (D_RWKV, D_RWKV), 1) // HEAD
    return jnp.where(r == c, scale, 0.0).astype(F32)


def _s5_params_kernel(lre_ref, lim_ref, step_ref, bre_ref, bim_ref,
                      are_o, aim_o, bbre_o, bbim_o):
    lam_re = jnp.minimum(lre_ref[...], -1e-4)
    lam_im = lim_ref[...]
    dt = jnp.exp(step_ref[...])
    mag = jnp.exp(lam_re * dt)
    lb_re = mag * jnp.cos(lam_im * dt)
    lb_im = mag * jnp.sin(lam_im * dt)
    den = lam_re * lam_re + lam_im * lam_im
    num_re = lb_re - 1.0
    z_re = (num_re * lam_re + lb_im * lam_im) / den
    z_im = (lb_im * lam_re - num_re * lam_im) / den
    are_o[...] = lb_re
    aim_o[...] = lb_im
    pm = SSM_STATE * SSM_GROUP
    rep = (lax.broadcasted_iota(jnp.int32, (SSM_STATE, pm), 0)
           == lax.broadcasted_iota(jnp.int32, (SSM_STATE, pm), 1) // SSM_GROUP)
    rep = jnp.where(rep, 1.0, 0.0).astype(F32)
    zr = _mm_f32_lhs(z_re, rep)
    zi = _mm_f32_lhs(z_im, rep)
    b_re = bre_ref[...]
    b_im = bim_ref[...]
    bbre_o[...] = zr * b_re - zi * b_im
    bbim_o[...] = zr * b_im + zi * b_re


def _s5_params(lambda_re, lambda_im, log_step, b_re, b_im):
    g, p, m = b_re.shape
    outs = pl.pallas_call(
        _s5_params_kernel,
        out_shape=(jax.ShapeDtypeStruct((g, p), F32), jax.ShapeDtypeStruct((g, p), F32),
                   jax.ShapeDtypeStruct((g, p * m), F32), jax.ShapeDtypeStruct((g, p * m), F32)),
        name="s5_params",
    )(lambda_re, lambda_im, log_step.reshape(g, 1), b_re.reshape(g, p * m), b_im.reshape(g, p * m))
    a_re, a_im, bb_re, bb_im = outs
    return a_re, a_im, bb_re.reshape(g, p, m), bb_im.reshape(g, p, m)


def _inproj_kernel(x_ref, g1_ref, win_ref, mu_ref, w0_ref, wup_ref, a0_ref, aup_ref, gup_ref,
                   kk_ref, ka_ref,
                   r_o, lw_o, k_o, v_o, kk_o, b_o, g_o, u_o, carry_ref):
    ti = pl.program_id(1)
    h = _rms(x_ref[0], g1_ref[...])
    proj = _mm(h, win_ref[...])
    u_o[0] = proj[:, N_SHIFT:]
    p = proj[:, :N_SHIFT]
    tt = p.shape[0]
    carry = jnp.where(ti == 0, 0.0, carry_ref[0:1, :])
    row = lax.broadcasted_iota(jnp.int32, p.shape, 0)
    prev = jnp.where(row == 0, carry, pltpu.roll(p, 1, 0))
    carry_ref[0:1, :] = p[tt - 1:tt, :]
    ps = p + mu_ref[...] * (prev - p)

    r = ps[:, 0:D_RWKV]
    k = ps[:, D_RWKV:2 * D_RWKV]
    v = ps[:, 2 * D_RWKV:3 * D_RWKV]
    lora = ps[:, 3 * D_RWKV:3 * D_RWKV + LORA_W + LORA_A]
    g_lr = ps[:, 3 * D_RWKV + LORA_W + LORA_A:]

    z = w0_ref[...] + _mm(jnp.tanh(lora), wup_ref[...])
    nz = -z
    softplus = jnp.maximum(nz, 0.0) + jnp.log(1.0 + jnp.exp(-jnp.abs(nz)))
    log_w = -softplus - 0.5
    lw = -jnp.exp(log_w)
    a = _sigmoid(a0_ref[...] + _mm(lora, aup_ref[...]))
    g = _mm(_sigmoid(g_lr), gup_ref[...])

    kk = k * kk_ref[...]
    ss = _mm_f32_lhs(kk * kk, _head_sum_matrix(1.0))
    kk = kk / jnp.maximum(jnp.sqrt(ss), 1e-12)
    k2 = k * (1.0 + (a - 1.0) * ka_ref[...])

    r_o[0] = r
    lw_o[0] = lw
    k_o[0] = k2
    v_o[0] = v
    kk_o[0] = kk
    b_o[0] = kk * a
    g_o[0] = g


def _inproj(x, norm1_g, w_in, mu_shift, w0, w_up, a0, a_up, g_up, k_k, k_a, tt):
    bsz, t_len, d = x.shape
    zeros_w = jnp.zeros((LORA_A, D_RWKV), F32)
    zeros_a = jnp.zeros((LORA_W, D_RWKV), F32)
    wup_pad = jnp.concatenate([w_up, zeros_w], axis=0).astype(_MXU)
    aup_pad = jnp.concatenate([zeros_a, a_up], axis=0).astype(_MXU)
    row = lambda v: v.reshape(1, -1)
    full = lambda shape: pl.BlockSpec(shape, lambda b, t: (0,) * len(shape))
    tok = lambda w: pl.BlockSpec((1, tt, w), lambda b, t: (b, t, 0))
    out_sds = jax.ShapeDtypeStruct((bsz, t_len, D_RWKV), F32)
    return pl.pallas_call(
        _inproj_kernel,
        grid=(bsz, t_len // tt),
        in_specs=[tok(d), full((1, d)), full((d, D_IN)), full((1, N_SHIFT)), full((1, D_RWKV)),
                  full((LORA_W + LORA_A, D_RWKV)), full((1, D_RWKV)), full((LORA_W + LORA_A, D_RWKV)),
                  full((LORA_G, D_RWKV)), full((1, D_RWKV)), full((1, D_RWKV))],
        out_specs=[tok(D_RWKV)] * 8,
        out_shape=[out_sds] * 8,
        scratch_shapes=[pltpu.VMEM((8, N_SHIFT), F32)],
        compiler_params=pltpu.CompilerParams(
            dimension_semantics=("arbitrary", "arbitrary"), vmem_limit_bytes=VMEM_LIMIT),
        name="inproj",
    )(x, row(norm1_g), w_in.astype(_MXU), row(mu_shift), row(w0), wup_pad, row(a0), aup_pad,
      g_up.astype(_MXU), row(k_k), row(k_a))


def _blockdiag(m, head0):
    return jnp.concatenate([jnp.where(head0, m, 0.0), jnp.where(head0, 0.0, m)], axis=0)


def _rwkv_kernel(r_ref, lw_ref, k_ref, v_ref, kk_ref, b_ref, g_ref, rk_ref, lnw_ref, lnb_ref,
                 y_o, h_ref, y_scr):
    tb = pl.program_id(1)
    tc = r_ref.shape[1]
    n_chunks = tc // CHUNK

    @pl.when(tb == 0)
    def _():
        h_ref[...] = jnp.zeros_like(h_ref)

    row = lax.broadcasted_iota(jnp.int32, (CHUNK, PAIR), 0)
    lane = lax.broadcasted_iota(jnp.int32, (CHUNK, PAIR), 1)
    head0 = lane < HEAD
    src = lane % HEAD
    strict = src < row
    incl = src <= row
    eye_pair = jnp.where(src == row, 1.0, 0.0).astype(F32)
    row2 = lax.broadcasted_iota(jnp.int32, (PAIR, PAIR), 0)
    lane2 = lax.broadcasted_iota(jnp.int32, (PAIR, PAIR), 1)
    diag2 = row2 == lane2
    same_head = (row2 < HEAD) == (lane2 < HEAD)
    tri = (lax.broadcasted_iota(jnp.int32, (CHUNK, CHUNK), 1)
           <= lax.broadcasted_iota(jnp.int32, (CHUNK, CHUNK), 0))
    tri = jnp.where(tri, 1.0, 0.0).astype(F32)

    def chunk_body(c, carry):
        rows = pl.ds(pl.multiple_of(c * CHUNK, CHUNK), CHUNK)
        lw = lw_ref[0, rows, :]
        cs = _mm_f32_rhs(tri, lw)
        cs_last = cs[CHUNK - 1:CHUNK, :]
        e_pos = jnp.exp(cs)
        e_neg = jnp.exp(-cs)
        e_end = jnp.exp(cs_last - cs)
        r = r_ref[0, rows, :]
        k = k_ref[0, rows, :]
        v = v_ref[0, rows, :]
        kk = kk_ref[0, rows, :]
        b = b_ref[0, rows, :]
        r_t = r * e_pos
        a_t = -kk * jnp.exp(cs - lw)
        b_t = b * e_neg
        k_t = k * e_neg
        b_h = b * e_end
        k_h = k * e_end
        g_last = jnp.exp(cs_last)

        for p in range(N_PAIRS):
            sl = slice(p * PAIR, (p + 1) * PAIR)
            gram = _mm_nt(jnp.concatenate([a_t[:, sl], r_t[:, sl]], axis=0),
                          jnp.concatenate([_blockdiag(b_t[:, sl], head0),
                                           _blockdiag(k_t[:, sl], head0)], axis=0))
            a_ab = jnp.where(strict, gram[:CHUNK, :PAIR], 0.0)
            a_ak = jnp.where(strict, gram[:CHUNK, PAIR:], 0.0)
            a_rb = jnp.where(incl, gram[CHUNK:, :PAIR], 0.0)
            a_rk = jnp.where(incl, gram[CHUNK:, PAIR:], 0.0)

            t_inv = eye_pair + a_ab
            m_pow = _mm(a_ab, _blockdiag(a_ab, head0))
            n_levels = CHUNK.bit_length() - 2
            for level in range(n_levels):
                if level + 1 < n_levels:
                    out = _mm(m_pow, jnp.concatenate(
                        [_blockdiag(t_inv, head0), _blockdiag(m_pow, head0)], axis=1))
                    t_inv = t_inv + out[:, :PAIR]
                    m_pow = out[:, PAIR:]
                else:
                    t_inv = t_inv + _mm(m_pow, _blockdiag(t_inv, head0))

            vp = v[:, sl]
            av = _mm(jnp.concatenate([a_ak, a_rk], axis=0), _blockdiag(vp, head0))
            w12 = _mm(t_inv, jnp.concatenate(
                [_blockdiag(a_t[:, sl], head0), _blockdiag(av[:CHUNK], head0)], axis=1))
            h0 = h_ref[p]
            uy = _mm(jnp.concatenate([w12[:, :PAIR], r_t[:, sl]], axis=0), h0)
            u = uy[:CHUNK] + w12[:, PAIR:]
            y = uy[CHUNK:] + av[CHUNK:] + _mm(a_rb, _blockdiag(u, head0))
            y_scr[rows, sl] = y
            decay = jnp.where(diag2, g_last[:, sl], 0.0)
            h_new = _mm_tn(jnp.concatenate([b_h[:, sl], k_h[:, sl], decay], axis=0),
                           jnp.concatenate([u, vp, h0], axis=0))
            h_ref[p] = jnp.where(same_head, h_new, 0.0)
        return carry

    lax.fori_loop(0, n_chunks, chunk_body, 0)

    y = y_scr[...]
    mean_m = _head_sum_matrix(1.0 / HEAD)
    mu = _mm_f32_lhs(y, mean_m)
    d = y - mu
    var = _mm_f32_lhs(d * d, mean_m)
    yn = d * lax.rsqrt(var + LN_X_EPS) * lnw_ref[...] + lnb_ref[...]
    r = r_ref[0]
    bonus = _mm_f32_lhs(r * k_ref[0] * rk_ref[...], _head_sum_matrix(1.0)) * v_ref[0]
    y_o[0] = (yn + bonus) * g_ref[0]


def _rwkv(r, lw, k, v, kk, b, g, r_k, ln_w, ln_b, tc):
    bsz, t_len, _ = r.shape
    tok = pl.BlockSpec((1, tc, D_RWKV), lambda bb, t: (bb, t, 0))
    par = pl.BlockSpec((1, D_RWKV), lambda bb, t: (0, 0))
    return pl.pallas_call(
        _rwkv_kernel,
        grid=(bsz, t_len // tc),
        in_specs=[tok] * 7 + [par] * 3,
        out_specs=tok,
        out_shape=jax.ShapeDtypeStruct((bsz, t_len, D_RWKV), F32),
        scratch_shapes=[pltpu.VMEM((N_PAIRS, PAIR, PAIR), F32), pltpu.VMEM((tc, D_RWKV), F32)],
        compiler_params=pltpu.CompilerParams(
            dimension_semantics=("arbitrary", "arbitrary"), vmem_limit_bytes=VMEM_LIMIT),
        name="rwkv",
    )(r, lw, k, v, kk, b, g, r_k.reshape(1, D_RWKV), ln_w.reshape(1, D_RWKV), ln_b.reshape(1, D_RWKV))


def _s5_kernel(u_ref, are_ref, aim_ref, bbd_ref, cbd_ref, dskip_ref, wglu_ref, bglu_ref, beta_ref,
               y_o, sre_ref, sim_ref, bu_ref):
    i = pl.program_id(0)
    lc, bsz, _ = u_ref.shape

    @pl.when(i == 0)
    def _():
        sre_ref[...] = jnp.zeros_like(sre_ref)
        sim_ref[...] = jnp.zeros_like(sim_ref)

    u = u_ref[...].reshape(lc * bsz, D_SSM)
    bu_ref[...] = _mm(u, bbd_ref[...])
    a_re = jnp.broadcast_to(are_ref[...], (bsz, N_STATE))
    a_im = jnp.broadcast_to(aim_ref[...], (bsz, N_STATE))

    def step(t, carry):
        s_re, s_im = carry
        rows = pl.ds(pl.multiple_of(t * bsz, bsz), bsz)
        n_re = a_re * s_re - a_im * s_im + bu_ref[rows, :N_STATE]
        n_im = a_re * s_im + a_im * s_re + bu_ref[rows, N_STATE:]
        bu_ref[rows, :N_STATE] = n_re
        bu_ref[rows, N_STATE:] = n_im
        return n_re, n_im

    s_re, s_im = lax.fori_loop(0, lc, step, (sre_ref[...], sim_ref[...]))
    sre_ref[...] = s_re
    sim_ref[...] = s_im

    y = _mm(bu_ref[...], cbd_ref[...]) + dskip_ref[...] * u
    y = jax.nn.gelu(y)
    y = y * _sigmoid(_mm(y, wglu_ref[...]) + bglu_ref[...])
    y_o[...] = _rms(y, beta_ref[...]).reshape(lc, bsz, D_SSM)


def _s5(u_tm, a_re, a_im, bb_re, bb_im, c_re, c_im, d_skip, w_glu, b_glu, beta, lc):
    t_len, bsz, _ = u_tm.shape
    eye = jnp.eye(N_GROUPS, dtype=F32)
    bbd = jnp.concatenate(
        [jnp.einsum('gpm,gh->gmhp', bb_re, eye).reshape(D_SSM, N_STATE),
         jnp.einsum('gpm,gh->gmhp', bb_im, eye).reshape(D_SSM, N_STATE)], axis=1).astype(_MXU)
    cbd = jnp.concatenate(
        [jnp.einsum('gmp,gh->gphm', c_re, eye).reshape(N_STATE, D_SSM),
         jnp.einsum('gmp,gh->gphm', -c_im, eye).reshape(N_STATE, D_SSM)], axis=0).astype(_MXU)
    full = lambda shape: pl.BlockSpec(shape, lambda i: (0,) * len(shape))
    tok = pl.BlockSpec((lc, bsz, D_SSM), lambda i: (i, 0, 0))
    return pl.pallas_call(
        _s5_kernel,
        grid=(t_len // lc,),
        in_specs=[tok, full((1, N_STATE)), full((1, N_STATE)), full((D_SSM, 2 * N_STATE)),
                  full((2 * N_STATE, D_SSM)), full((1, D_SSM)), full((D_SSM, D_SSM)),
                  full((1, D_SSM)), full((1, D_SSM))],
        out_specs=tok,
        out_shape=jax.ShapeDtypeStruct((t_len, bsz, D_SSM), F32),
        scratch_shapes=[pltpu.VMEM((bsz, N_STATE), F32), pltpu.VMEM((bsz, N_STATE), F32),
                        pltpu.VMEM((lc * bsz, 2 * N_STATE), F32)],
        compiler_params=pltpu.CompilerParams(
            dimension_semantics=("arbitrary",), vmem_limit_bytes=VMEM_LIMIT),
        name="s5",
    )(u_tm, a_re.reshape(1, N_STATE), a_im.reshape(1, N_STATE), bbd, cbd, d_skip.reshape(1, D_SSM),
      w_glu.astype(_MXU), b_glu.reshape(1, D_SSM), beta.reshape(1, D_SSM))


ROUTE_IDX = 0
ROUTE_RANK = TOP_K
ROUTE_GATE = 2 * TOP_K


def _outproj_kernel(x_ref, yr_ref, ys_ref, wout_ref, g2_ref, rw_ref, rb_ref,
                    x2_o, h2_o, route_o, cnt_o, carry_ref):
    first = jnp.logical_and(pl.program_id(0) == 0, pl.program_id(1) == 0)

    @pl.when(first)
    def _():
        carry_ref[...] = jnp.zeros_like(carry_ref)

    x2 = (x_ref[0] + _mm(yr_ref[0], wout_ref[:D_RWKV, :]) + _mm(ys_ref[0], wout_ref[D_RWKV:, :]))
    x2_o[0] = x2
    h2 = _rms(x2, g2_ref[...])
    h2_o[0] = h2
    logits = _mm_hi(h2, rw_ref[...]) + rb_ref[...]
    tt = logits.shape[0]
    lane = lax.broadcasted_iota(jnp.int32, (tt, LANES), 1).astype(F32)

    vals, idxs = [], []
    member = jnp.zeros((tt, LANES), F32)
    for _ in range(TOP_K):
        m = jnp.max(logits, axis=-1, keepdims=True)
        idx = jnp.min(jnp.where(logits == m, lane, float(LANES)), axis=-1, keepdims=True)
        sel = lane == idx
        member = jnp.where(sel, 1.0, member)
        logits = jnp.where(sel, -jnp.inf, logits)
        vals.append(m)
        idxs.append(idx)
    exps = [jnp.exp(v - vals[0]) for v in vals]
    denom = exps[0] + exps[1] + exps[2] + exps[3]

    before = (lax.broadcasted_iota(jnp.int32, (tt, tt), 1)
              < lax.broadcasted_iota(jnp.int32, (tt, tt), 0))
    carry = carry_ref[...]
    rank_all = _mm(jnp.where(before, 1.0, 0.0), member) + carry
    carry = carry + jnp.sum(member, axis=0, keepdims=True)
    carry_ref[...] = carry
    cnt_o[...] = jnp.broadcast_to(carry, cnt_o.shape)

    route = jnp.zeros((tt, LANES), F32)
    for k in range(TOP_K):
        rank_k = jnp.sum(jnp.where(lane == idxs[k], rank_all, 0.0), axis=-1, keepdims=True)
        route = jnp.where(lane == ROUTE_IDX + k, idxs[k], route)
        route = jnp.where(lane == ROUTE_RANK + k, rank_k, route)
        route = jnp.where(lane == ROUTE_GATE + k, exps[k] / denom, route)
    route_o[0] = route


def _outproj(x, y_rwkv, y_ssm, w_out, norm2_g, router_w, router_b, tt):
    bsz, t_len, d = x.shape
    rw = jnp.zeros((d, LANES), F32).at[:, :N_EXPERTS].set(router_w)
    rb = jnp.full((1, LANES), -1e30, F32).at[0, :N_EXPERTS].set(router_b)
    full = lambda shape: pl.BlockSpec(shape, lambda b, t: (0,) * len(shape))
    tok = lambda w: pl.BlockSpec((1, tt, w), lambda b, t: (b, t, 0))
    return pl.pallas_call(
        _outproj_kernel,
        grid=(bsz, t_len // tt),
        in_specs=[tok(d), tok(D_RWKV), tok(D_SSM), full((d, d)), full((1, d)), full((d, LANES)),
                  full((1, LANES))],
        out_specs=[tok(d), tok(d), tok(LANES), full((8, LANES))],
        out_shape=[jax.ShapeDtypeStruct((bsz, t_len, d), F32), jax.ShapeDtypeStruct((bsz, t_len, d), F32),
                   jax.ShapeDtypeStruct((bsz, t_len, LANES), F32), jax.ShapeDtypeStruct((8, LANES), F32)],
        scratch_shapes=[pltpu.VMEM((1, LANES), F32)],
        compiler_params=pltpu.CompilerParams(
            dimension_semantics=("arbitrary", "arbitrary"), vmem_limit_bytes=VMEM_LIMIT),
        name="outproj_router",
    )(x, y_rwkv, y_ssm, w_out.astype(_MXU), norm2_g.reshape(1, d), rw, rb)


def _row_copy_dispatch(h_ref, xs_ref, sem, i, p):
    return pltpu.make_async_copy(h_ref.at[pl.ds(i, 1)], xs_ref.at[pl.ds(p, 1)], sem)


def _dispatch_kernel(pos_hbm, h_ref, xs_in, xs_o, pos_smem, pos_sem, sem):
    del xs_in
    i = pl.program_id(0)
    tt = h_ref.shape[0]
    cp = pltpu.make_async_copy(pos_hbm.at[i], pos_smem, pos_sem)
    cp.start()
    cp.wait()

    def issue(t, carry):
        for k in range(TOP_K):
            _row_copy_dispatch(h_ref, xs_o, sem, t, pos_smem[t * TOP_K + k]).start()
        return carry

    lax.fori_loop(0, tt, issue, 0)
    for _ in range(TOP_K):
        pltpu.make_async_copy(h_ref, xs_o.at[pl.ds(0, tt)], sem).wait()


def _dispatch(pos_blocks, h2, n_rows, tt):
    n, d = h2.shape
    xs0 = jnp.zeros((n_rows, d), F32)
    return pl.pallas_call(
        _dispatch_kernel,
        grid=(n // tt,),
        in_specs=[pl.BlockSpec(memory_space=pl.ANY), pl.BlockSpec((tt, d), lambda i: (i, 0)),
                  pl.BlockSpec(memory_space=pl.ANY)],
        out_specs=pl.BlockSpec(memory_space=pl.ANY),
        out_shape=jax.ShapeDtypeStruct((n_rows, d), F32),
        scratch_shapes=[pltpu.SMEM((tt * TOP_K,), jnp.int32), pltpu.SemaphoreType.DMA,
                        pltpu.SemaphoreType.DMA],
        input_output_aliases={2: 0},
        compiler_params=pltpu.CompilerParams(dimension_semantics=("arbitrary",)),
        name="moe_dispatch",
    )(pos_blocks, h2, xs0)


def _ffn_kernel(te_ref, nused_ref, xs_ref, w1g_ref, w1l_ref, b1g_ref, b1l_ref, w2_ref, b2_ref, ys_o):
    i = pl.program_id(0)

    @pl.when(i < nused_ref[0])
    def _():
        xb = xs_ref[...].astype(_MXU)
        x_glu = jnp.minimum(_mm(xb, w1g_ref[0]) + b1g_ref[0], SWIGLU_LIMIT)
        x_lin = jnp.clip(_mm(xb, w1l_ref[0]) + b1l_ref[0], -SWIGLU_LIMIT, SWIGLU_LIMIT)
        act = x_glu * _sigmoid(SWIGLU_ALPHA * x_glu) * (x_lin + 1.0)
        ys_o[...] = _mm(act, w2_ref[0]) + b2_ref[0]

    @pl.when(i >= nused_ref[0])
    def _():
        ys_o[...] = jnp.zeros_like(ys_o)


def _ffn(tile_expert, n_used, xs, w1g, w1l, b1g, b1l, w2, b2):
    n_rows, d = xs.shape
    n_tiles = n_rows // FFN_TILE

    def row_map(i, te, nu):
        return (jnp.minimum(i, nu[0] - 1), 0)

    def exp_map(i, te, nu):
        return (te[i], 0, 0)

    return pl.pallas_call(
        _ffn_kernel,
        grid_spec=pltpu.PrefetchScalarGridSpec(
            num_scalar_prefetch=2,
            grid=(n_tiles,),
            in_specs=[pl.BlockSpec((FFN_TILE, d), row_map),
                      pl.BlockSpec((1, d, D_FF), exp_map), pl.BlockSpec((1, d, D_FF), exp_map),
                      pl.BlockSpec((1, 1, D_FF), exp_map), pl.BlockSpec((1, 1, D_FF), exp_map),
                      pl.BlockSpec((1, D_FF, d), exp_map), pl.BlockSpec((1, 1, d), exp_map)],
            out_specs=pl.BlockSpec((FFN_TILE, d), lambda i, te, nu: (i, 0))),
        out_shape=jax.ShapeDtypeStruct((n_rows, d), F32),
        compiler_params=pltpu.CompilerParams(
            dimension_semantics=("arbitrary",), vmem_limit_bytes=VMEM_LIMIT),
        name="moe_ffn",
    )(tile_expert, n_used, xs, w1g, w1l, b1g, b1l, w2, b2)


def _row_copy_combine(ys_ref, buf_ref, sem, k, i, p):
    return pltpu.make_async_copy(ys_ref.at[pl.ds(p, 1)], buf_ref.at[k, pl.ds(i, 1)], sem)


def _combine_kernel(pos_hbm, x2_ref, route_ref, ys_hbm, fg_ref, out_o, buf_ref, pos_smem, pos_sem, sem):
    i = pl.program_id(0)
    tt = x2_ref.shape[0]
    cp = pltpu.make_async_copy(pos_hbm.at[i], pos_smem, pos_sem)
    cp.start()
    cp.wait()

    def issue(t, carry):
        for k in range(TOP_K):
            _row_copy_combine(ys_hbm, buf_ref, sem, k, t, pos_smem[t * TOP_K + k]).start()
        return carry

    lax.fori_loop(0, tt, issue, 0)
    for k in range(TOP_K):
        pltpu.make_async_copy(ys_hbm.at[pl.ds(0, tt)], buf_ref.at[k], sem).wait()

    route = route_ref[...]
    acc = x2_ref[...]
    for k in range(TOP_K):
        acc = acc + route[:, ROUTE_GATE + k:ROUTE_GATE + k + 1] * buf_ref[k]
    out_o[...] = _rms(acc, fg_ref[...])


def _combine(pos_blocks, x2, route, ys, final_g, tt):
    n, d = x2.shape
    return pl.pallas_call(
        _combine_kernel,
        grid=(n // tt,),
        in_specs=[pl.BlockSpec(memory_space=pl.ANY), pl.BlockSpec((tt, d), lambda i: (i, 0)),
                  pl.BlockSpec((tt, LANES), lambda i: (i, 0)), pl.BlockSpec(memory_space=pl.ANY),
                  pl.BlockSpec((1, d), lambda i: (0, 0))],
        out_specs=pl.BlockSpec((tt, d), lambda i: (i, 0)),
        out_shape=jax.ShapeDtypeStruct((n, d), F32),
        scratch_shapes=[pltpu.VMEM((TOP_K, tt, d), F32), pltpu.SMEM((tt * TOP_K,), jnp.int32),
                        pltpu.SemaphoreType.DMA, pltpu.SemaphoreType.DMA],
        compiler_params=pltpu.CompilerParams(
            dimension_semantics=("arbitrary",), vmem_limit_bytes=VMEM_LIMIT),
        name="moe_combine",
    )(pos_blocks, x2, route, ys, final_g.reshape(1, d))


def _moe(x2, h2, route, counts, w1, b1, w2, b2, final_g, tt):
    n, d = x2.shape
    cnt = counts.astype(jnp.int32)
    padded = (cnt + FFN_TILE - 1) // FFN_TILE * FFN_TILE
    ends = jnp.cumsum(padded)
    offs = ends - padded
    eidx = route[:, ROUTE_IDX:ROUTE_IDX + TOP_K].astype(jnp.int32)
    rank = route[:, ROUTE_RANK:ROUTE_RANK + TOP_K].astype(jnp.int32)
    pos = (offs[eidx] + rank).reshape(n // tt, tt * TOP_K)
    n_rows = n * TOP_K + N_EXPERTS * FFN_TILE
    n_tiles = n_rows // FFN_TILE
    tile_start = jnp.arange(n_tiles, dtype=jnp.int32) * FFN_TILE
    tile_expert = jnp.minimum(jnp.searchsorted(ends, tile_start, side='right'),
                              N_EXPERTS - 1).astype(jnp.int32)
    n_used = (ends[-1:] // FFN_TILE).astype(jnp.int32)

    xs = _dispatch(pos, h2, n_rows, tt)
    w1g = w1[:, :, 0::2].astype(_MXU)
    w1l = w1[:, :, 1::2].astype(_MXU)
    b1g = b1[:, None, 0::2]
    b1l = b1[:, None, 1::2]
    ys = _ffn(tile_expert, n_used, xs, w1g, w1l, b1g, b1l, w2.astype(_MXU), b2[:, None, :])
    return _combine(pos, x2, route, ys, final_g, tt)


def _tile(t_len, want):
    return want if t_len % want == 0 else t_len


def kernel(x, norm1_g, w_in, mu_shift, w0, w_up, a0, a_up, g_up, k_k, k_a, r_k, ln_x_w, ln_x_b,
           lambda_re, lambda_im, log_step, b_re, b_im, c_re, c_im, d_skip, w_glu, b_glu, beta_ssm,
           w_out, norm2_g, router_w, router_b, w1, b1, w2, b2, final_g):
    bsz, t_len, d = x.shape
    assert d == D_MODEL and t_len % CHUNK == 0 and norm1_g.shape[0] == 1
    tt = _tile(t_len, 256)
    a_re, a_im, bb_re, bb_im = _s5_params(lambda_re[0], lambda_im[0], log_step[0], b_re[0], b_im[0])
    r, lw, k, v, kk, b, g, u = _inproj(x, norm1_g[0], w_in[0], mu_shift[0], w0[0], w_up[0], a0[0],
                                       a_up[0], g_up[0], k_k[0], k_a[0], tt)
    y_rwkv = _rwkv(r, lw, k, v, kk, b, g, r_k[0], ln_x_w[0], ln_x_b[0], _tile(t_len, 512))
    y_ssm = _s5(jnp.swapaxes(u, 0, 1), a_re, a_im, bb_re, bb_im, c_re[0], c_im[0], d_skip[0],
                w_glu[0], b_glu[0], beta_ssm[0], _tile(t_len, 64))
    y_ssm = jnp.swapaxes(y_ssm, 0, 1)
    x2, h2, route, counts = _outproj(x, y_rwkv, y_ssm, w_out[0], norm2_g[0], router_w[0],
                                     router_b[0], _tile(t_len, 512))
    n = bsz * t_len
    out = _moe(x2.reshape(n, d), h2.reshape(n, d), route.reshape(n, LANES), counts[0, :N_EXPERTS],
               w1[0], b1[0], w2[0], b2[0], final_g, _tile(n, 512))
    return out.reshape(bsz, t_len, d)
```

```python
import functools

import jax
import jax.numpy as jnp
from jax import lax
from jax.experimental import pallas as pl
from jax.experimental.pallas import tpu as pltpu

F32 = jnp.float32
BF16 = jnp.bfloat16
_MXU = jnp.bfloat16

D_MODEL = 1024
D_RWKV = 512
D_SSM = 512
HEAD = 64
N_HEADS = D_RWKV // HEAD
LORA_W = 64
LORA_A = 64
LORA_G = 128
N_SHIFT = 3 * D_RWKV + LORA_W + LORA_A + LORA_G
D_IN = N_SHIFT + D_SSM
SSM_GROUP = 16
N_GROUPS = D_SSM // SSM_GROUP
SSM_STATE = 64
N_STATE = N_GROUPS * SSM_STATE
N_EXPERTS = 32
TOP_K = 4
D_FF = D_MODEL
SWIGLU_ALPHA = 1.702
SWIGLU_LIMIT = 7.0
NORM_EPS = 1e-5
LN_X_EPS = 64e-5

LANES = 128
CHUNK = 64
PAIR = 2 * HEAD
N_PAIRS = D_RWKV // PAIR
FFN_TILE = 512
VMEM_LIMIT = 56 * 1024 * 1024


def _mm(a, b):
    return jnp.dot(a.astype(_MXU), b.astype(_MXU), preferred_element_type=F32)


def _mm_nt(a, b):
    return lax.dot_general(a.astype(_MXU), b.astype(_MXU), (((1,), (1,)), ((), ())),
                           preferred_element_type=F32)


def _mm_tn(a, b):
    return lax.dot_general(a.astype(_MXU), b.astype(_MXU), (((0,), (0,)), ((), ())),
                           preferred_element_type=F32)


def _split3(a):
    h1 = a.astype(_MXU)
    r = a - h1.astype(F32)
    h2 = r.astype(_MXU)
    r = r - h2.astype(F32)
    return h1, h2, r.astype(_MXU)


def _mm_f32_lhs(a, m):
    mb = m.astype(_MXU)
    h1, h2, h3 = _split3(a)
    out = jnp.dot(h1, mb, preferred_element_type=F32)
    out = out + jnp.dot(h2, mb, preferred_element_type=F32)
    return out + jnp.dot(h3, mb, preferred_element_type=F32)


def _mm_f32_rhs(m, b):
    mb = m.astype(_MXU)
    h1, h2, h3 = _split3(b)
    out = jnp.dot(mb, h1, preferred_element_type=F32)
    out = out + jnp.dot(mb, h2, preferred_element_type=F32)
    return out + jnp.dot(mb, h3, preferred_element_type=F32)


def _mm_hi(a, b):
    a1 = a.astype(_MXU)
    a2 = (a - a1.astype(F32)).astype(_MXU)
    b1 = b.astype(_MXU)
    b2 = (b - b1.astype(F32)).astype(_MXU)
    out = jnp.dot(a1, b1, preferred_element_type=F32)
    out = out + jnp.dot(a1, b2, preferred_element_type=F32)
    return out + jnp.dot(a2, b1, preferred_element_type=F32)


def _sigmoid(z):
    return 1.0 / (1.0 + jnp.exp(-z))


def _rms(x, g):
    ms = jnp.mean(x * x, axis=-1, keepdims=True)
    return x * lax.rsqrt(ms + NORM_EPS) * g


MXU_TILE = 256


def _head_sum(x, scale):
    r = lax.broadcasted_iota(jnp.int32, (MXU_TILE, MXU_TILE), 0) // HEAD
    c = lax.broadcasted_iota(jnp.int32, (MXU_TILE, MXU_TILE), 1) // HEAD
    m = jnp.where(r == c, scale, 0.0).astype(_MXU)
    hi = x.astype(_MXU)
    lo = (x - hi.astype(F32)).astype(_MXU)
    parts = []
    for j in range(x.shape[1] // MXU_TILE):
        sl = slice(j * MXU_TILE, (j + 1) * MXU_TILE)
        parts.append(jnp.dot(hi[:, sl], m, preferred_element_type=F32)
                     + jnp.dot(lo[:, sl], m, preferred_element_type=F32))
    return jnp.concatenate(parts, axis=1)


def _s5_params_kernel(lre_ref, lim_ref, step_ref, bre_ref, bim_ref,
                      are_o, aim_o, bbre_o, bbim_o):
    lam_re = jnp.minimum(lre_ref[...], -1e-4)
    lam_im = lim_ref[...]
    dt = jnp.exp(step_ref[...])
    mag = jnp.exp(lam_re * dt)
    lb_re = mag * jnp.cos(lam_im * dt)
    lb_im = mag * jnp.sin(lam_im * dt)
    den = lam_re * lam_re + lam_im * lam_im
    num_re = lb_re - 1.0
    z_re = (num_re * lam_re + lb_im * lam_im) / den
    z_im = (lb_im * lam_re - num_re * lam_im) / den
    are_o[...] = lb_re
    aim_o[...] = lb_im
    pm = SSM_STATE * SSM_GROUP
    rep = (lax.broadcasted_iota(jnp.int32, (SSM_STATE, pm), 0)
           == lax.broadcasted_iota(jnp.int32, (SSM_STATE, pm), 1) // SSM_GROUP)
    rep = jnp.where(rep, 1.0, 0.0).astype(F32)
    zr = _mm_f32_lhs(z_re, rep)
    zi = _mm_f32_lhs(z_im, rep)
    b_re = bre_ref[...]
    b_im = bim_ref[...]
    bbre_o[...] = zr * b_re - zi * b_im
    bbim_o[...] = zr * b_im + zi * b_re


def _s5_params(lambda_re, lambda_im, log_step, b_re, b_im):
    g, p, m = b_re.shape
    outs = pl.pallas_call(
        _s5_params_kernel,
        out_shape=(jax.ShapeDtypeStruct((g, p), F32), jax.ShapeDtypeStruct((g, p), F32),
                   jax.ShapeDtypeStruct((g, p * m), F32), jax.ShapeDtypeStruct((g, p * m), F32)),
        name="s5_params",
    )(lambda_re, lambda_im, log_step.reshape(g, 1), b_re.reshape(g, p * m), b_im.reshape(g, p * m))
    a_re, a_im, bb_re, bb_im = outs
    return a_re, a_im, bb_re.reshape(g, p, m), bb_im.reshape(g, p, m)


def _inproj_kernel(x_ref, g1_ref, win_ref, mu_ref, w0_ref, wup_ref, a0_ref, aup_ref, gup_ref,
                   kk_ref, ka_ref,
                   r_o, lw_o, k_o, v_o, kk_o, b_o, g_o, u_o, carry_ref):
    ti = pl.program_id(1)
    h = _rms(x_ref[0], g1_ref[...])
    proj = _mm(h, win_ref[...])
    u_o[0] = proj[:, N_SHIFT:]
    p = proj[:, :N_SHIFT]
    tt = p.shape[0]
    carry = jnp.where(ti == 0, 0.0, carry_ref[0:1, :])
    row = lax.broadcasted_iota(jnp.int32, p.shape, 0)
    prev = jnp.where(row == 0, carry, pltpu.roll(p, 1, 0))
    carry_ref[0:1, :] = p[tt - 1:tt, :]
    ps = p + mu_ref[...] * (prev - p)

    r = ps[:, 0:D_RWKV]
    k = ps[:, D_RWKV:2 * D_RWKV]
    v = ps[:, 2 * D_RWKV:3 * D_RWKV]
    lora = ps[:, 3 * D_RWKV:3 * D_RWKV + LORA_W + LORA_A]
    g_lr = ps[:, 3 * D_RWKV + LORA_W + LORA_A:]

    z = w0_ref[...] + _mm(jnp.tanh(lora), wup_ref[...])
    nz = -z
    softplus = jnp.maximum(nz, 0.0) + jnp.log(1.0 + jnp.exp(-jnp.abs(nz)))
    log_w = -softplus - 0.5
    lw = -jnp.exp(log_w)
    a = _sigmoid(a0_ref[...] + _mm(lora, aup_ref[...]))
    g = _mm(_sigmoid(g_lr), gup_ref[...])

    kk = k * kk_ref[...]
    kk = kk / jnp.maximum(jnp.sqrt(_head_sum(kk * kk, 1.0)), 1e-12)
    k2 = k * (1.0 + (a - 1.0) * ka_ref[...])

    r_o[0] = r
    lw_o[0] = lw
    k_o[0] = k2
    v_o[0] = v
    kk_o[0] = kk
    b_o[0] = kk * a
    g_o[0] = g


def _inproj(x, norm1_g, w_in, mu_shift, w0, w_up, a0, a_up, g_up, k_k, k_a, tt):
    bsz, t_len, d = x.shape
    zeros_w = jnp.zeros((LORA_A, D_RWKV), F32)
    zeros_a = jnp.zeros((LORA_W, D_RWKV), F32)
    wup_pad = jnp.concatenate([w_up, zeros_w], axis=0).astype(_MXU)
    aup_pad = jnp.concatenate([zeros_a, a_up], axis=0).astype(_MXU)
    row = lambda v: v.reshape(1, -1)
    full = lambda shape: pl.BlockSpec(shape, lambda b, t: (0,) * len(shape))
    tok = lambda w: pl.BlockSpec((1, tt, w), lambda b, t: (b, t, 0))
    out_sds = jax.ShapeDtypeStruct((bsz, t_len, D_RWKV), F32)
    return pl.pallas_call(
        _inproj_kernel,
        grid=(bsz, t_len // tt),
        in_specs=[tok(d), full((1, d)), full((d, D_IN)), full((1, N_SHIFT)), full((1, D_RWKV)),
                  full((LORA_W + LORA_A, D_RWKV)), full((1, D_RWKV)), full((LORA_W + LORA_A, D_RWKV)),
                  full((LORA_G, D_RWKV)), full((1, D_RWKV)), full((1, D_RWKV))],
        out_specs=[tok(D_RWKV)] * 8,
        out_shape=[out_sds] * 8,
        scratch_shapes=[pltpu.VMEM((8, N_SHIFT), F32)],
        compiler_params=pltpu.CompilerParams(
            dimension_semantics=("arbitrary", "arbitrary"), vmem_limit_bytes=VMEM_LIMIT),
        name="inproj",
    )(x, row(norm1_g), w_in.astype(_MXU), row(mu_shift), row(w0), wup_pad, row(a0), aup_pad,
      g_up.astype(_MXU), row(k_k), row(k_a))


def _blockdiag(m, head0):
    return jnp.concatenate([jnp.where(head0, m, 0.0), jnp.where(head0, 0.0, m)], axis=0)


def _rwkv_kernel(r_ref, lw_ref, k_ref, v_ref, kk_ref, b_ref, g_ref, rk_ref, lnw_ref, lnb_ref,
                 y_o, s0_ref, s1_ref, s2_ref, s3_ref, y_scr,
                 w1_s, rt_s, arb_s, bh_s, kh_s, w2_s, ypar_s, gl_s):
    tb = pl.program_id(1)
    tc = r_ref.shape[1]
    n_chunks = tc // CHUNK

    @pl.when(tb == 0)
    def _():
        for s_ref in (s0_ref, s1_ref, s2_ref, s3_ref):
            s_ref[...] = jnp.zeros_like(s_ref)

    row = lax.broadcasted_iota(jnp.int32, (CHUNK, PAIR), 0)
    lane = lax.broadcasted_iota(jnp.int32, (CHUNK, PAIR), 1)
    head0 = lane < HEAD
    src = lane % HEAD
    strict = src < row
    incl = src <= row
    eye_pair = jnp.where(src == row, 1.0, 0.0).astype(F32)
    row2 = lax.broadcasted_iota(jnp.int32, (PAIR, PAIR), 0)
    lane2 = lax.broadcasted_iota(jnp.int32, (PAIR, PAIR), 1)
    same_head = (row2 < HEAD) == (lane2 < HEAD)
    tri = (lax.broadcasted_iota(jnp.int32, (CHUNK, CHUNK), 1)
           <= lax.broadcasted_iota(jnp.int32, (CHUNK, CHUNK), 0))
    tri = jnp.where(tri, 1.0, 0.0).astype(F32)

    pairs = range(N_PAIRS)
    psl = [slice(p * PAIR, (p + 1) * PAIR) for p in pairs]

    def prep_body(c, carry):
        rows = pl.ds(pl.multiple_of(c * CHUNK, CHUNK), CHUNK)
        lw = lw_ref[0, rows, :]
        cs = _mm_f32_rhs(tri, lw)
        cs_last = cs[CHUNK - 1:CHUNK, :]
        e_pos = jnp.exp(cs)
        e_neg = jnp.exp(-cs)
        e_end = jnp.exp(cs_last - cs)
        r = r_ref[0, rows, :]
        k = k_ref[0, rows, :]
        v = v_ref[0, rows, :]
        b = b_ref[0, rows, :]
        r_t = r * e_pos
        a_t = -kk_ref[0, rows, :] * jnp.exp(cs - lw)
        b_t = b * e_neg
        k_t = k * e_neg
        rt_s[rows, :] = r_t.astype(rt_s.dtype)
        bh_s[rows, :] = (b * e_end).astype(bh_s.dtype)
        kh_s[rows, :] = (k * e_end).astype(kh_s.dtype)
        g_rows = pl.ds(pl.multiple_of(c * 8, 8), 8)
        gl_s[g_rows, :] = jnp.broadcast_to(jnp.exp(cs_last), (8, D_RWKV))

        gram = [_mm_nt(jnp.concatenate([a_t[:, s], r_t[:, s]], axis=0),
                       jnp.concatenate([_blockdiag(b_t[:, s], head0),
                                        _blockdiag(k_t[:, s], head0)], axis=0)) for s in psl]
        a_ab = [jnp.where(strict, gm[:CHUNK, :PAIR], 0.0) for gm in gram]
        a_ak = [jnp.where(strict, gm[:CHUNK, PAIR:], 0.0) for gm in gram]
        a_rb = [jnp.where(incl, gm[CHUNK:, :PAIR], 0.0) for gm in gram]
        a_rk = [jnp.where(incl, gm[CHUNK:, PAIR:], 0.0) for gm in gram]
        for p in pairs:
            arb_s[rows, psl[p]] = a_rb[p].astype(arb_s.dtype)
        av = [_mm(jnp.concatenate([a_ak[p], a_rk[p]], axis=0), _blockdiag(v[:, psl[p]], head0))
              for p in pairs]
        for p in pairs:
            ypar_s[rows, psl[p]] = av[p][CHUNK:]

        t_inv = [eye_pair + a_ab[p] for p in pairs]
        m_pow = [_mm(a_ab[p], _blockdiag(a_ab[p], head0)) for p in pairs]
        n_levels = CHUNK.bit_length() - 2
        for level in range(n_levels):
            if level + 1 < n_levels:
                out = [_mm(m_pow[p], jnp.concatenate(
                    [_blockdiag(t_inv[p], head0), _blockdiag(m_pow[p], head0)], axis=1)) for p in pairs]
                t_inv = [t_inv[p] + out[p][:, :PAIR] for p in pairs]
                m_pow = [out[p][:, PAIR:] for p in pairs]
            else:
                t_inv = [t_inv[p] + _mm(m_pow[p], _blockdiag(t_inv[p], head0)) for p in pairs]
        w12 = [_mm(t_inv[p], jnp.concatenate(
            [_blockdiag(a_t[:, psl[p]], head0), _blockdiag(av[p][:CHUNK], head0)], axis=1)) for p in pairs]
        for p in pairs:
            w1_s[rows, psl[p]] = w12[p][:, :PAIR].astype(w1_s.dtype)
            w2_s[rows, psl[p]] = w12[p][:, PAIR:]
        return carry

    lax.fori_loop(0, n_chunks, prep_body, 0, unroll=2)

    s_refs = (s0_ref, s1_ref, s2_ref, s3_ref)

    def state_body(c, carry):
        rows = pl.ds(pl.multiple_of(c * CHUNK, CHUNK), CHUNK)
        g_rows = pl.ds(pl.multiple_of(c * 8, 8), 8)
        s0 = [s_refs[p][...] for p in pairs]
        uy = [_mm_nt(jnp.concatenate([w1_s[rows, psl[p]], rt_s[rows, psl[p]]], axis=0), s0[p])
              for p in pairs]
        u = [uy[p][:CHUNK] + w2_s[rows, psl[p]] for p in pairs]
        y = [uy[p][CHUNK:] + ypar_s[rows, psl[p]] + _mm(arb_s[rows, psl[p]], _blockdiag(u[p], head0))
             for p in pairs]
        upd = [_mm_tn(jnp.concatenate([u[p], v_ref[0, rows, psl[p]]], axis=0),
                      jnp.concatenate([bh_s[rows, psl[p]], kh_s[rows, psl[p]]], axis=0)) for p in pairs]
        for p in pairs:
            y_scr[rows, psl[p]] = y[p]
            decay = jnp.concatenate([gl_s[g_rows, psl[p]]] * (PAIR // 8), axis=0)
            s_refs[p][...] = s0[p] * decay + jnp.where(same_head, upd[p], 0.0)
        return carry

    lax.fori_loop(0, n_chunks, state_body, 0)

    y = y_scr[...]
    mu = _head_sum(y, 1.0 / HEAD)
    d = y - mu
    var = _head_sum(d * d, 1.0 / HEAD)
    yn = d * lax.rsqrt(var + LN_X_EPS) * lnw_ref[...] + lnb_ref[...]
    bonus = _head_sum(r_ref[0] * k_ref[0] * rk_ref[...], 1.0) * v_ref[0]
    y_o[0] = (yn + bonus) * g_ref[0]


def _rwkv(r, lw, k, v, kk, b, g, r_k, ln_w, ln_b, tc):
    bsz, t_len, _ = r.shape
    tok = pl.BlockSpec((1, tc, D_RWKV), lambda bb, t: (bb, t, 0))
    par = pl.BlockSpec((1, D_RWKV), lambda bb, t: (0, 0))
    return pl.pallas_call(
        _rwkv_kernel,
        grid=(bsz, t_len // tc),
        in_specs=[tok] * 7 + [par] * 3,
        out_specs=tok,
        out_shape=jax.ShapeDtypeStruct((bsz, t_len, D_RWKV), F32),
        scratch_shapes=([pltpu.VMEM((PAIR, PAIR), F32)] * N_PAIRS + [pltpu.VMEM((tc, D_RWKV), F32)]
                        + [pltpu.VMEM((tc, D_RWKV), _MXU)] * 5 + [pltpu.VMEM((tc, D_RWKV), F32)] * 2
                        + [pltpu.VMEM((tc // CHUNK * 8, D_RWKV), F32)]),
        compiler_params=pltpu.CompilerParams(
            dimension_semantics=("arbitrary", "arbitrary"), vmem_limit_bytes=VMEM_LIMIT),
        name="rwkv",
    )(r, lw, k, v, kk, b, g, r_k.reshape(1, D_RWKV), ln_w.reshape(1, D_RWKV), ln_b.reshape(1, D_RWKV))


def _s5_kernel(u_ref, are_ref, aim_ref, bbd_ref, cbd_ref, dskip_ref, wglu_ref, bglu_ref, beta_ref,
               y_o, sre_ref, sim_ref, bu_ref):
    i = pl.program_id(0)
    lc, bsz, _ = u_ref.shape

    @pl.when(i == 0)
    def _():
        sre_ref[...] = jnp.zeros_like(sre_ref)
        sim_ref[...] = jnp.zeros_like(sim_ref)

    u = u_ref[...].reshape(lc * bsz, D_SSM)
    bu_ref[...] = _mm(u, bbd_ref[...])
    a_re = jnp.broadcast_to(are_ref[...], (bsz, N_STATE))
    a_im = jnp.broadcast_to(aim_ref[...], (bsz, N_STATE))

    def step(t, carry):
        s_re, s_im = carry
        rows = pl.ds(pl.multiple_of(t * bsz, bsz), bsz)
        n_re = a_re * s_re - a_im * s_im + bu_ref[rows, :N_STATE]
        n_im = a_re * s_im + a_im * s_re + bu_ref[rows, N_STATE:]
        bu_ref[rows, :N_STATE] = n_re
        bu_ref[rows, N_STATE:] = n_im
        return n_re, n_im

    s_re, s_im = lax.fori_loop(0, lc, step, (sre_ref[...], sim_ref[...]))
    sre_ref[...] = s_re
    sim_ref[...] = s_im

    y = _mm(bu_ref[...], cbd_ref[...]) + dskip_ref[...] * u
    y = jax.nn.gelu(y)
    y = y * _sigmoid(_mm(y, wglu_ref[...]) + bglu_ref[...])
    y_o[...] = _rms(y, beta_ref[...]).reshape(lc, bsz, D_SSM)


def _s5(u_tm, a_re, a_im, bb_re, bb_im, c_re, c_im, d_skip, w_glu, b_glu, beta, lc):
    t_len, bsz, _ = u_tm.shape
    eye = jnp.eye(N_GROUPS, dtype=F32)
    bbd = jnp.concatenate(
        [jnp.einsum('gpm,gh->gmhp', bb_re, eye).reshape(D_SSM, N_STATE),
         jnp.einsum('gpm,gh->gmhp', bb_im, eye).reshape(D_SSM, N_STATE)], axis=1).astype(_MXU)
    cbd = jnp.concatenate(
        [jnp.einsum('gmp,gh->gphm', c_re, eye).reshape(N_STATE, D_SSM),
         jnp.einsum('gmp,gh->gphm', -c_im, eye).reshape(N_STATE, D_SSM)], axis=0).astype(_MXU)
    full = lambda shape: pl.BlockSpec(shape, lambda i: (0,) * len(shape))
    tok = pl.BlockSpec((lc, bsz, D_SSM), lambda i: (i, 0, 0))
    return pl.pallas_call(
        _s5_kernel,
        grid=(t_len // lc,),
        in_specs=[tok, full((1, N_STATE)), full((1, N_STATE)), full((D_SSM, 2 * N_STATE)),
                  full((2 * N_STATE, D_SSM)), full((1, D_SSM)), full((D_SSM, D_SSM)),
                  full((1, D_SSM)), full((1, D_SSM))],
        out_specs=tok,
        out_shape=jax.ShapeDtypeStruct((t_len, bsz, D_SSM), F32),
        scratch_shapes=[pltpu.VMEM((bsz, N_STATE), F32), pltpu.VMEM((bsz, N_STATE), F32),
                        pltpu.VMEM((lc * bsz, 2 * N_STATE), F32)],
        compiler_params=pltpu.CompilerParams(
            dimension_semantics=("arbitrary",), vmem_limit_bytes=VMEM_LIMIT),
        name="s5",
    )(u_tm, a_re.reshape(1, N_STATE), a_im.reshape(1, N_STATE), bbd, cbd, d_skip.reshape(1, D_SSM),
      w_glu.astype(_MXU), b_glu.reshape(1, D_SSM), beta.reshape(1, D_SSM))


ROUTE_IDX = 0
ROUTE_RANK = TOP_K
ROUTE_GATE = 2 * TOP_K


def _outproj_kernel(x_ref, yr_ref, ys_ref, wout_ref, g2_ref, rw_ref, rb_ref,
                    x2_o, h2_o, route_o, cnt_o, carry_ref):
    first = jnp.logical_and(pl.program_id(0) == 0, pl.program_id(1) == 0)

    @pl.when(first)
    def _():
        carry_ref[...] = jnp.zeros_like(carry_ref)

    x2 = (x_ref[0] + _mm(yr_ref[0], wout_ref[:D_RWKV, :]) + _mm(ys_ref[0], wout_ref[D_RWKV:, :]))
    x2_o[0] = x2
    h2 = _rms(x2, g2_ref[...])
    h2_o[0] = h2
    logits = _mm_hi(h2, rw_ref[...]) + rb_ref[...]
    tt = logits.shape[0]
    lane = lax.broadcasted_iota(jnp.int32, (tt, LANES), 1).astype(F32)

    vals, idxs = [], []
    member = jnp.zeros((tt, LANES), F32)
    for _ in range(TOP_K):
        m = jnp.max(logits, axis=-1, keepdims=True)
        idx = jnp.min(jnp.where(logits == m, lane, float(LANES)), axis=-1, keepdims=True)
        sel = lane == idx
        member = jnp.where(sel, 1.0, member)
        logits = jnp.where(sel, -jnp.inf, logits)
        vals.append(m)
        idxs.append(idx)
    exps = [jnp.exp(v - vals[0]) for v in vals]
    denom = exps[0] + exps[1] + exps[2] + exps[3]

    before = (lax.broadcasted_iota(jnp.int32, (tt, tt), 1)
              < lax.broadcasted_iota(jnp.int32, (tt, tt), 0))
    carry = carry_ref[...]
    rank_all = _mm(jnp.where(before, 1.0, 0.0), member) + carry
    carry = carry + jnp.sum(member, axis=0, keepdims=True)
    carry_ref[...] = carry
    cnt_o[...] = jnp.broadcast_to(carry, cnt_o.shape)

    route = jnp.zeros((tt, LANES), F32)
    for k in range(TOP_K):
        rank_k = jnp.sum(jnp.where(lane == idxs[k], rank_all, 0.0), axis=-1, keepdims=True)
        route = jnp.where(lane == ROUTE_IDX + k, idxs[k], route)
        route = jnp.where(lane == ROUTE_RANK + k, rank_k, route)
        route = jnp.where(lane == ROUTE_GATE + k, exps[k] / denom, route)
    route_o[0] = route


def _outproj(x, y_rwkv, y_ssm, w_out, norm2_g, router_w, router_b, tt):
    bsz, t_len, d = x.shape
    rw = jnp.zeros((d, LANES), F32).at[:, :N_EXPERTS].set(router_w)
    rb = jnp.full((1, LANES), -1e30, F32).at[0, :N_EXPERTS].set(router_b)
    full = lambda shape: pl.BlockSpec(shape, lambda b, t: (0,) * len(shape))
    tok = lambda w: pl.BlockSpec((1, tt, w), lambda b, t: (b, t, 0))
    return pl.pallas_call(
        _outproj_kernel,
        grid=(bsz, t_len // tt),
        in_specs=[tok(d), tok(D_RWKV), tok(D_SSM), full((d, d)), full((1, d)), full((d, LANES)),
                  full((1, LANES))],
        out_specs=[tok(d), tok(d), tok(LANES), full((8, LANES))],
        out_shape=[jax.ShapeDtypeStruct((bsz, t_len, d), F32), jax.ShapeDtypeStruct((bsz, t_len, d), F32),
                   jax.ShapeDtypeStruct((bsz, t_len, LANES), F32), jax.ShapeDtypeStruct((8, LANES), F32)],
        scratch_shapes=[pltpu.VMEM((1, LANES), F32)],
        compiler_params=pltpu.CompilerParams(
            dimension_semantics=("arbitrary", "arbitrary"), vmem_limit_bytes=VMEM_LIMIT),
        name="outproj_router",
    )(x, y_rwkv, y_ssm, w_out.astype(_MXU), norm2_g.reshape(1, d), rw, rb)


def _row_copy_dispatch(h_ref, xs_ref, sem, i, p):
    return pltpu.make_async_copy(h_ref.at[pl.ds(i, 1)], xs_ref.at[pl.ds(p, 1)], sem)


def _zero_tile_copy(z_ref, xs_ref, sem, start):
    return pltpu.make_async_copy(z_ref, xs_ref.at[pl.ds(start, FFN_TILE)], sem)


def _dispatch_kernel(ends_ref, padded_ref, pos_hbm, h_ref, xs_o, pos_smem, z_ref, pos_sem, sem, z_sem):
    i = pl.program_id(0)
    tt = h_ref.shape[0]
    cp = pltpu.make_async_copy(pos_hbm.at[i], pos_smem, pos_sem)
    cp.start()

    @pl.when(i == 0)
    def _():
        z_ref[...] = jnp.zeros_like(z_ref)
        for e in range(N_EXPERTS):
            @pl.when(padded_ref[e] > 0)
            def _():
                start = pl.multiple_of(ends_ref[e] - FFN_TILE, FFN_TILE)
                _zero_tile_copy(z_ref, xs_o, z_sem, start).start()
        n_used = ends_ref[N_EXPERTS - 1] // FFN_TILE
        n_tiles = xs_o.shape[0] // FFN_TILE

        def zero_tail(j, carry):
            _zero_tile_copy(z_ref, xs_o, z_sem, pl.multiple_of(j * FFN_TILE, FFN_TILE)).start()
            return carry

        def wait_tail(j, carry):
            _zero_tile_copy(z_ref, xs_o, z_sem, 0).wait()
            return carry

        lax.fori_loop(n_used, n_tiles, zero_tail, 0)
        for e in range(N_EXPERTS):
            @pl.when(padded_ref[e] > 0)
            def _():
                _zero_tile_copy(z_ref, xs_o, z_sem, 0).wait()
        lax.fori_loop(n_used, n_tiles, wait_tail, 0)

    cp.wait()

    def issue(t, carry):
        for k in range(TOP_K):
            _row_copy_dispatch(h_ref, xs_o, sem, t, pos_smem[t * TOP_K + k]).start()
        return carry

    lax.fori_loop(0, tt, issue, 0)
    for _ in range(TOP_K):
        pltpu.make_async_copy(h_ref, xs_o.at[pl.ds(0, tt)], sem).wait()


def _dispatch(ends, padded, pos_blocks, h2, n_rows, tt):
    n, d = h2.shape
    return pl.pallas_call(
        _dispatch_kernel,
        grid_spec=pltpu.PrefetchScalarGridSpec(
            num_scalar_prefetch=2,
            grid=(n // tt,),
            in_specs=[pl.BlockSpec(memory_space=pl.ANY), pl.BlockSpec((tt, d), lambda i, en, pa: (i, 0))],
            out_specs=pl.BlockSpec(memory_space=pl.ANY),
            scratch_shapes=[pltpu.SMEM((tt * TOP_K,), jnp.int32), pltpu.VMEM((FFN_TILE, d), F32),
                            pltpu.SemaphoreType.DMA, pltpu.SemaphoreType.DMA, pltpu.SemaphoreType.DMA]),
        out_shape=jax.ShapeDtypeStruct((n_rows, d), F32),
        compiler_params=pltpu.CompilerParams(dimension_semantics=("arbitrary",)),
        name="moe_dispatch",
    )(ends, padded, pos_blocks, h2)


def _w1_perm_kernel(w_ref, o_ref):
    r = lax.broadcasted_iota(jnp.int32, (MXU_TILE, MXU_TILE), 0)
    c = lax.broadcasted_iota(jnp.int32, (MXU_TILE, MXU_TILE), 1)
    half = MXU_TILE // 2
    src = jnp.where(c < half, 2 * c, 2 * (c - half) + 1)
    perm = jnp.where(r == src, 1.0, 0.0).astype(_MXU)
    for j in range(w_ref.shape[2] // MXU_TILE):
        sl = slice(j * MXU_TILE, (j + 1) * MXU_TILE)
        o_ref[0, :, sl] = jnp.dot(w_ref[0, :, sl].astype(_MXU), perm,
                                  preferred_element_type=F32).astype(o_ref.dtype)


def _w1_perm(w1):
    e, d, f2 = w1.shape
    spec = pl.BlockSpec((1, d, f2), lambda i: (i, 0, 0))
    return pl.pallas_call(
        _w1_perm_kernel,
        grid=(e,),
        in_specs=[spec],
        out_specs=spec,
        out_shape=jax.ShapeDtypeStruct(w1.shape, _MXU),
        compiler_params=pltpu.CompilerParams(
            dimension_semantics=("arbitrary",), vmem_limit_bytes=VMEM_LIMIT),
        name="w1_regroup",
    )(w1)


def _ffn_kernel(te_ref, nused_ref, xs_ref, w1_ref, b1g_ref, b1l_ref, w2_ref, b2_ref, ys_o):
    i = pl.program_id(0)

    @pl.when(i < nused_ref[0])
    def _():
        hu = _mm(xs_ref[...], w1_ref[0])
        half = MXU_TILE // 2
        n_grp = hu.shape[1] // MXU_TILE
        h_glu = jnp.concatenate([hu[:, j * MXU_TILE:j * MXU_TILE + half] for j in range(n_grp)], axis=1)
        h_lin = jnp.concatenate([hu[:, j * MXU_TILE + half:(j + 1) * MXU_TILE] for j in range(n_grp)], axis=1)
        x_glu = jnp.minimum(h_glu + b1g_ref[0], SWIGLU_LIMIT)
        x_lin = jnp.clip(h_lin + b1l_ref[0], -SWIGLU_LIMIT, SWIGLU_LIMIT)
        act = x_glu * _sigmoid(SWIGLU_ALPHA * x_glu) * (x_lin + 1.0)
        ys_o[...] = _mm(act, w2_ref[0]) + b2_ref[0]

    @pl.when(i >= nused_ref[0])
    def _():
        ys_o[...] = jnp.zeros_like(ys_o)


def _ffn(tile_expert, n_used, xs, w1p, b1g, b1l, w2, b2):
    n_rows, d = xs.shape
    n_tiles = n_rows // FFN_TILE

    def row_map(i, te, nu):
        return (jnp.minimum(i, nu[0] - 1), 0)

    def exp_map(i, te, nu):
        return (te[i], 0, 0)

    return pl.pallas_call(
        _ffn_kernel,
        grid_spec=pltpu.PrefetchScalarGridSpec(
            num_scalar_prefetch=2,
            grid=(n_tiles,),
            in_specs=[pl.BlockSpec((FFN_TILE, d), row_map),
                      pl.BlockSpec((1, d, 2 * D_FF), exp_map),
                      pl.BlockSpec((1, 1, D_FF), exp_map), pl.BlockSpec((1, 1, D_FF), exp_map),
                      pl.BlockSpec((1, D_FF, d), exp_map), pl.BlockSpec((1, 1, d), exp_map)],
            out_specs=pl.BlockSpec((FFN_TILE, d), lambda i, te, nu: (i, 0))),
        out_shape=jax.ShapeDtypeStruct((n_rows, d), F32),
        compiler_params=pltpu.CompilerParams(
            dimension_semantics=("arbitrary",), vmem_limit_bytes=VMEM_LIMIT),
        name="moe_ffn",
    )(tile_expert, n_used, xs, w1p, b1g, b1l, w2, b2)


def _row_copy_combine(ys_ref, buf_ref, sem, k, i, p):
    return pltpu.make_async_copy(ys_ref.at[pl.ds(p, 1)], buf_ref.at[k, pl.ds(i, 1)], sem)


def _combine_kernel(pos_hbm, x2_ref, route_ref, ys_hbm, fg_ref, out_o, buf_ref, pos_smem, pos_sem, sem):
    i = pl.program_id(0)
    tt = x2_ref.shape[0]
    cp = pltpu.make_async_copy(pos_hbm.at[i], pos_smem, pos_sem)
    cp.start()
    cp.wait()

    def issue(t, carry):
        for k in range(TOP_K):
            _row_copy_combine(ys_hbm, buf_ref, sem, k, t, pos_smem[t * TOP_K + k]).start()
        return carry

    lax.fori_loop(0, tt, issue, 0)
    for k in range(TOP_K):
        pltpu.make_async_copy(ys_hbm.at[pl.ds(0, tt)], buf_ref.at[k], sem).wait()

    route = route_ref[...]
    acc = x2_ref[...]
    for k in range(TOP_K):
        acc = acc + route[:, ROUTE_GATE + k:ROUTE_GATE + k + 1] * buf_ref[k]
    out_o[...] = _rms(acc, fg_ref[...])


def _combine(pos_blocks, x2, route, ys, final_g, tt):
    n, d = x2.shape
    return pl.pallas_call(
        _combine_kernel,
        grid=(n // tt,),
        in_specs=[pl.BlockSpec(memory_space=pl.ANY), pl.BlockSpec((tt, d), lambda i: (i, 0)),
                  pl.BlockSpec((tt, LANES), lambda i: (i, 0)), pl.BlockSpec(memory_space=pl.ANY),
                  pl.BlockSpec((1, d), lambda i: (0, 0))],
        out_specs=pl.BlockSpec((tt, d), lambda i: (i, 0)),
        out_shape=jax.ShapeDtypeStruct((n, d), F32),
        scratch_shapes=[pltpu.VMEM((TOP_K, tt, d), F32), pltpu.SMEM((tt * TOP_K,), jnp.int32),
                        pltpu.SemaphoreType.DMA, pltpu.SemaphoreType.DMA],
        compiler_params=pltpu.CompilerParams(
            dimension_semantics=("arbitrary",), vmem_limit_bytes=VMEM_LIMIT),
        name="moe_combine",
    )(pos_blocks, x2, route, ys, final_g.reshape(1, d))


def _moe(x2, h2, route, counts, w1, b1, w2, b2, final_g, tt):
    n, d = x2.shape
    cnt = counts.astype(jnp.int32)
    padded = (cnt + FFN_TILE - 1) // FFN_TILE * FFN_TILE
    ends = jnp.cumsum(padded)
    offs = ends - padded
    eidx = route[:, ROUTE_IDX:ROUTE_IDX + TOP_K].astype(jnp.int32)
    rank = route[:, ROUTE_RANK:ROUTE_RANK + TOP_K].astype(jnp.int32)
    experts = jnp.arange(N_EXPERTS, dtype=jnp.int32)
    off_of = jnp.sum(jnp.where(eidx[..., None] == experts, offs, 0), axis=-1)
    pos = (off_of + rank).reshape(n // tt, tt * TOP_K)
    n_rows = n * TOP_K + N_EXPERTS * FFN_TILE
    n_tiles = n_rows // FFN_TILE
    tile_start = jnp.arange(n_tiles, dtype=jnp.int32) * FFN_TILE
    tile_expert = jnp.minimum(jnp.sum((tile_start[:, None] >= ends[None, :]).astype(jnp.int32), axis=1),
                              N_EXPERTS - 1)
    n_used = ends[-1:] // FFN_TILE

    xs = _dispatch(ends, padded, pos, h2, n_rows, tt)
    ys = _ffn(tile_expert, n_used, xs, _w1_perm(w1), b1[:, None, 0::2], b1[:, None, 1::2],
              w2.astype(_MXU), b2[:, None, :])
    return _combine(pos, x2, route, ys, final_g, tt)


def _tile(t_len, want):
    return want if t_len % want == 0 else t_len


def kernel(x, norm1_g, w_in, mu_shift, w0, w_up, a0, a_up, g_up, k_k, k_a, r_k, ln_x_w, ln_x_b,
           lambda_re, lambda_im, log_step, b_re, b_im, c_re, c_im, d_skip, w_glu, b_glu, beta_ssm,
           w_out, norm2_g, router_w, router_b, w1, b1, w2, b2, final_g):
    bsz, t_len, d = x.shape
    assert d == D_MODEL and t_len % CHUNK == 0 and norm1_g.shape[0] == 1
    tt = _tile(t_len, 256)
    a_re, a_im, bb_re, bb_im = _s5_params(lambda_re[0], lambda_im[0], log_step[0], b_re[0], b_im[0])
    r, lw, k, v, kk, b, g, u = _inproj(x, norm1_g[0], w_in[0], mu_shift[0], w0[0], w_up[0], a0[0],
                                       a_up[0], g_up[0], k_k[0], k_a[0], tt)
    y_rwkv = _rwkv(r, lw, k, v, kk, b, g, r_k[0], ln_x_w[0], ln_x_b[0], _tile(t_len, 512))
    y_ssm = _s5(jnp.swapaxes(u, 0, 1), a_re, a_im, bb_re, bb_im, c_re[0], c_im[0], d_skip[0],
                w_glu[0], b_glu[0], beta_ssm[0], _tile(t_len, 64))
    y_ssm = jnp.swapaxes(y_ssm, 0, 1)
    x2, h2, route, counts = _outproj(x, y_rwkv, y_ssm, w_out[0], norm2_g[0], router_w[0],
                                     router_b[0], _tile(t_len, 512))
    n = bsz * t_len
    out = _moe(x2.reshape(n, d), h2.reshape(n, d), route.reshape(n, LANES), counts[0, :N_EXPERTS],
               w1[0], b1[0], w2[0], b2[0], final_g, _tile(n, 512))
    return out.reshape(bsz, t_len, d)
```

```python
import functools

import jax
import jax.numpy as jnp
from jax import lax
from jax.experimental import pallas as pl
from jax.experimental.pallas import tpu as pltpu

F32 = jnp.float32
BF16 = jnp.bfloat16
_MXU = jnp.bfloat16

D_MODEL = 1024
D_RWKV = 512
D_SSM = 512
HEAD = 64
N_HEADS = D_RWKV // HEAD
LORA_W = 64
LORA_A = 64
LORA_G = 128
N_SHIFT = 3 * D_RWKV + LORA_W + LORA_A + LORA_G
D_IN = N_SHIFT + D_SSM
SSM_GROUP = 16
N_GROUPS = D_SSM // SSM_GROUP
SSM_STATE = 64
N_STATE = N_GROUPS * SSM_STATE
N_EXPERTS = 32
TOP_K = 4
D_FF = D_MODEL
SWIGLU_ALPHA = 1.702
SWIGLU_LIMIT = 7.0
NORM_EPS = 1e-5
LN_X_EPS = 64e-5

LANES = 128
CHUNK = 64
PAIR = 2 * HEAD
N_PAIRS = D_RWKV // PAIR
FFN_TILE = 512
VMEM_LIMIT = 56 * 1024 * 1024


def _mm(a, b):
    return jnp.dot(a.astype(_MXU), b.astype(_MXU), preferred_element_type=F32)


def _mm_nt(a, b):
    return lax.dot_general(a.astype(_MXU), b.astype(_MXU), (((1,), (1,)), ((), ())),
                           preferred_element_type=F32)


def _mm_tn(a, b):
    return lax.dot_general(a.astype(_MXU), b.astype(_MXU), (((0,), (0,)), ((), ())),
                           preferred_element_type=F32)


def _split3(a):
    h1 = a.astype(_MXU)
    r = a - h1.astype(F32)
    h2 = r.astype(_MXU)
    r = r - h2.astype(F32)
    return h1, h2, r.astype(_MXU)


def _mm_f32_lhs(a, m):
    mb = m.astype(_MXU)
    h1, h2, h3 = _split3(a)
    out = jnp.dot(h1, mb, preferred_element_type=F32)
    out = out + jnp.dot(h2, mb, preferred_element_type=F32)
    return out + jnp.dot(h3, mb, preferred_element_type=F32)


def _mm_f32_rhs(m, b):
    mb = m.astype(_MXU)
    h1, h2, h3 = _split3(b)
    out = jnp.dot(mb, h1, preferred_element_type=F32)
    out = out + jnp.dot(mb, h2, preferred_element_type=F32)
    return out + jnp.dot(mb, h3, preferred_element_type=F32)


def _mm_hi(a, b):
    a1 = a.astype(_MXU)
    a2 = (a - a1.astype(F32)).astype(_MXU)
    b1 = b.astype(_MXU)
    b2 = (b - b1.astype(F32)).astype(_MXU)
    out = jnp.dot(a1, b1, preferred_element_type=F32)
    out = out + jnp.dot(a1, b2, preferred_element_type=F32)
    return out + jnp.dot(a2, b1, preferred_element_type=F32)


def _sigmoid(z):
    return 1.0 / (1.0 + jnp.exp(-z))


def _rms(x, g):
    ms = jnp.mean(x * x, axis=-1, keepdims=True)
    return x * lax.rsqrt(ms + NORM_EPS) * g


MXU_TILE = 256


def _head_sum(x, scale):
    r = lax.broadcasted_iota(jnp.int32, (MXU_TILE, MXU_TILE), 0) // HEAD
    c = lax.broadcasted_iota(jnp.int32, (MXU_TILE, MXU_TILE), 1) // HEAD
    m = jnp.where(r == c, scale, 0.0).astype(_MXU)
    hi = x.astype(_MXU)
    lo = (x - hi.astype(F32)).astype(_MXU)
    parts = []
    for j in range(x.shape[1] // MXU_TILE):
        sl = slice(j * MXU_TILE, (j + 1) * MXU_TILE)
        parts.append(jnp.dot(hi[:, sl], m, preferred_element_type=F32)
                     + jnp.dot(lo[:, sl], m, preferred_element_type=F32))
    return jnp.concatenate(parts, axis=1)


def _s5_params_kernel(lre_ref, lim_ref, step_ref, bre_ref, bim_ref,
                      are_o, aim_o, bbre_o, bbim_o):
    lam_re = jnp.minimum(lre_ref[...], -1e-4)
    lam_im = lim_ref[...]
    dt = jnp.exp(step_ref[...])
    mag = jnp.exp(lam_re * dt)
    lb_re = mag * jnp.cos(lam_im * dt)
    lb_im = mag * jnp.sin(lam_im * dt)
    den = lam_re * lam_re + lam_im * lam_im
    num_re = lb_re - 1.0
    z_re = (num_re * lam_re + lb_im * lam_im) / den
    z_im = (lb_im * lam_re - num_re * lam_im) / den
    are_o[...] = lb_re
    aim_o[...] = lb_im
    pm = SSM_STATE * SSM_GROUP
    rep = (lax.broadcasted_iota(jnp.int32, (SSM_STATE, pm), 0)
           == lax.broadcasted_iota(jnp.int32, (SSM_STATE, pm), 1) // SSM_GROUP)
    rep = jnp.where(rep, 1.0, 0.0).astype(F32)
    zr = _mm_f32_lhs(z_re, rep)
    zi = _mm_f32_lhs(z_im, rep)
    b_re = bre_ref[...]
    b_im = bim_ref[...]
    bbre_o[...] = zr * b_re - zi * b_im
    bbim_o[...] = zr * b_im + zi * b_re


def _s5_params(lambda_re, lambda_im, log_step, b_re, b_im):
    g, p, m = b_re.shape
    outs = pl.pallas_call(
        _s5_params_kernel,
        out_shape=(jax.ShapeDtypeStruct((g, p), F32), jax.ShapeDtypeStruct((g, p), F32),
                   jax.ShapeDtypeStruct((g, p * m), F32), jax.ShapeDtypeStruct((g, p * m), F32)),
        name="s5_params",
    )(lambda_re, lambda_im, log_step.reshape(g, 1), b_re.reshape(g, p * m), b_im.reshape(g, p * m))
    a_re, a_im, bb_re, bb_im = outs
    return a_re, a_im, bb_re.reshape(g, p, m), bb_im.reshape(g, p, m)


def _inproj_kernel(x_ref, g1_ref, win_ref, mu_ref, w0_ref, wup_ref, a0_ref, aup_ref, gup_ref,
                   kk_ref, ka_ref,
                   r_o, lw_o, k_o, v_o, kk_o, b_o, g_o, u_o, carry_ref):
    ti = pl.program_id(1)
    h = _rms(x_ref[0], g1_ref[...])
    proj = _mm(h, win_ref[...])
    u_o[0] = proj[:, N_SHIFT:]
    p = proj[:, :N_SHIFT]
    tt = p.shape[0]
    carry = jnp.where(ti == 0, 0.0, carry_ref[0:1, :])
    row = lax.broadcasted_iota(jnp.int32, p.shape, 0)
    prev = jnp.where(row == 0, carry, pltpu.roll(p, 1, 0))
    carry_ref[0:1, :] = p[tt - 1:tt, :]
    ps = p + mu_ref[...] * (prev - p)

    r = ps[:, 0:D_RWKV]
    k = ps[:, D_RWKV:2 * D_RWKV]
    v = ps[:, 2 * D_RWKV:3 * D_RWKV]
    lora = ps[:, 3 * D_RWKV:3 * D_RWKV + LORA_W + LORA_A]
    g_lr = ps[:, 3 * D_RWKV + LORA_W + LORA_A:]

    z = w0_ref[...] + _mm(jnp.tanh(lora), wup_ref[...])
    nz = -z
    softplus = jnp.maximum(nz, 0.0) + jnp.log(1.0 + jnp.exp(-jnp.abs(nz)))
    log_w = -softplus - 0.5
    lw = -jnp.exp(log_w)
    a = _sigmoid(a0_ref[...] + _mm(lora, aup_ref[...]))
    g = _mm(_sigmoid(g_lr), gup_ref[...])

    kk = k * kk_ref[...]
    kk = kk / jnp.maximum(jnp.sqrt(_head_sum(kk * kk, 1.0)), 1e-12)
    k2 = k * (1.0 + (a - 1.0) * ka_ref[...])

    r_o[0] = r
    lw_o[0] = lw
    k_o[0] = k2
    v_o[0] = v
    kk_o[0] = kk
    b_o[0] = kk * a
    g_o[0] = g


def _inproj(x, norm1_g, w_in, mu_shift, w0, w_up, a0, a_up, g_up, k_k, k_a, tt):
    bsz, t_len, d = x.shape
    zeros_w = jnp.zeros((LORA_A, D_RWKV), F32)
    zeros_a = jnp.zeros((LORA_W, D_RWKV), F32)
    wup_pad = jnp.concatenate([w_up, zeros_w], axis=0).astype(_MXU)
    aup_pad = jnp.concatenate([zeros_a, a_up], axis=0).astype(_MXU)
    row = lambda v: v.reshape(1, -1)
    full = lambda shape: pl.BlockSpec(shape, lambda b, t: (0,) * len(shape))
    tok = lambda w: pl.BlockSpec((1, tt, w), lambda b, t: (b, t, 0))
    out_sds = jax.ShapeDtypeStruct((bsz, t_len, D_RWKV), F32)
    return pl.pallas_call(
        _inproj_kernel,
        grid=(bsz, t_len // tt),
        in_specs=[tok(d), full((1, d)), full((d, D_IN)), full((1, N_SHIFT)), full((1, D_RWKV)),
                  full((LORA_W + LORA_A, D_RWKV)), full((1, D_RWKV)), full((LORA_W + LORA_A, D_RWKV)),
                  full((LORA_G, D_RWKV)), full((1, D_RWKV)), full((1, D_RWKV))],
        out_specs=[tok(D_RWKV)] * 8,
        out_shape=[out_sds] * 8,
        scratch_shapes=[pltpu.VMEM((8, N_SHIFT), F32)],
        compiler_params=pltpu.CompilerParams(
            dimension_semantics=("arbitrary", "arbitrary"), vmem_limit_bytes=VMEM_LIMIT),
        name="inproj",
    )(x, row(norm1_g), w_in.astype(_MXU), row(mu_shift), row(w0), wup_pad, row(a0), aup_pad,
      g_up.astype(_MXU), row(k_k), row(k_a))


def _blockdiag(m, head0):
    return jnp.concatenate([jnp.where(head0, m, 0.0), jnp.where(head0, 0.0, m)], axis=0)


def _rwkv_kernel(r_ref, lw_ref, k_ref, v_ref, kk_ref, b_ref, g_ref, rk_ref, lnw_ref, lnb_ref,
                 y_o, s0_ref, s1_ref, s2_ref, s3_ref, y_scr,
                 w1_s, rt_s, arb_s, bh_s, kh_s, w2_s, ypar_s, gl_s):
    tb = pl.program_id(1)
    tc = r_ref.shape[1]
    n_chunks = tc // CHUNK

    @pl.when(tb == 0)
    def _():
        for s_ref in (s0_ref, s1_ref, s2_ref, s3_ref):
            s_ref[...] = jnp.zeros_like(s_ref)

    row = lax.broadcasted_iota(jnp.int32, (CHUNK, PAIR), 0)
    lane = lax.broadcasted_iota(jnp.int32, (CHUNK, PAIR), 1)
    head0 = lane < HEAD
    src = lane % HEAD
    strict = src < row
    incl = src <= row
    eye_pair = jnp.where(src == row, 1.0, 0.0).astype(F32)
    row2 = lax.broadcasted_iota(jnp.int32, (PAIR, PAIR), 0)
    lane2 = lax.broadcasted_iota(jnp.int32, (PAIR, PAIR), 1)
    same_head = (row2 < HEAD) == (lane2 < HEAD)
    tri = (lax.broadcasted_iota(jnp.int32, (CHUNK, CHUNK), 1)
           <= lax.broadcasted_iota(jnp.int32, (CHUNK, CHUNK), 0))
    tri = jnp.where(tri, 1.0, 0.0).astype(F32)

    pairs = range(N_PAIRS)
    psl = [slice(p * PAIR, (p + 1) * PAIR) for p in pairs]

    def prep_body(c, carry):
        rows = pl.ds(pl.multiple_of(c * CHUNK, CHUNK), CHUNK)
        lw = lw_ref[0, rows, :]
        cs = _mm_f32_rhs(tri, lw)
        cs_last = cs[CHUNK - 1:CHUNK, :]
        e_pos = jnp.exp(cs)
        e_neg = jnp.exp(-cs)
        e_end = jnp.exp(cs_last - cs)
        r = r_ref[0, rows, :]
        k = k_ref[0, rows, :]
        v = v_ref[0, rows, :]
        b = b_ref[0, rows, :]
        r_t = r * e_pos
        a_t = -kk_ref[0, rows, :] * jnp.exp(cs - lw)
        b_t = b * e_neg
        k_t = k * e_neg
        rt_s[rows, :] = r_t.astype(rt_s.dtype)
        bh_s[rows, :] = (b * e_end).astype(bh_s.dtype)
        kh_s[rows, :] = (k * e_end).astype(kh_s.dtype)
        g_rows = pl.ds(pl.multiple_of(c * 8, 8), 8)
        gl_s[g_rows, :] = jnp.broadcast_to(jnp.exp(cs_last), (8, D_RWKV))

        gram = [_mm_nt(jnp.concatenate([a_t[:, s], r_t[:, s]], axis=0),
                       jnp.concatenate([_blockdiag(b_t[:, s], head0),
                                        _blockdiag(k_t[:, s], head0)], axis=0)) for s in psl]
        a_ab = [jnp.where(strict, gm[:CHUNK, :PAIR], 0.0) for gm in gram]
        a_ak = [jnp.where(strict, gm[:CHUNK, PAIR:], 0.0) for gm in gram]
        a_rb = [jnp.where(incl, gm[CHUNK:, :PAIR], 0.0) for gm in gram]
        a_rk = [jnp.where(incl, gm[CHUNK:, PAIR:], 0.0) for gm in gram]
        for p in pairs:
            arb_s[rows, psl[p]] = a_rb[p].astype(arb_s.dtype)
        av = [_mm(jnp.concatenate([a_ak[p], a_rk[p]], axis=0), _blockdiag(v[:, psl[p]], head0))
              for p in pairs]
        for p in pairs:
            ypar_s[rows, psl[p]] = av[p][CHUNK:]

        t_inv = [eye_pair + a_ab[p] for p in pairs]
        m_pow = [_mm(a_ab[p], _blockdiag(a_ab[p], head0)) for p in pairs]
        n_levels = CHUNK.bit_length() - 2
        for level in range(n_levels):
            if level + 1 < n_levels:
                out = [_mm(m_pow[p], jnp.concatenate(
                    [_blockdiag(t_inv[p], head0), _blockdiag(m_pow[p], head0)], axis=1)) for p in pairs]
                t_inv = [t_inv[p] + out[p][:, :PAIR] for p in pairs]
                m_pow = [out[p][:, PAIR:] for p in pairs]
            else:
                t_inv = [t_inv[p] + _mm(m_pow[p], _blockdiag(t_inv[p], head0)) for p in pairs]
        w12 = [_mm(t_inv[p], jnp.concatenate(
            [_blockdiag(a_t[:, psl[p]], head0), _blockdiag(av[p][:CHUNK], head0)], axis=1)) for p in pairs]
        for p in pairs:
            w1_s[rows, psl[p]] = w12[p][:, :PAIR].astype(w1_s.dtype)
            w2_s[rows, psl[p]] = w12[p][:, PAIR:]
        return carry

    lax.fori_loop(0, n_chunks, prep_body, 0, unroll=2)

    s_refs = (s0_ref, s1_ref, s2_ref, s3_ref)

    def state_body(c, carry):
        rows = pl.ds(pl.multiple_of(c * CHUNK, CHUNK), CHUNK)
        g_rows = pl.ds(pl.multiple_of(c * 8, 8), 8)
        s0 = [s_refs[p][...] for p in pairs]
        uy = [_mm_nt(jnp.concatenate([w1_s[rows, psl[p]], rt_s[rows, psl[p]]], axis=0), s0[p])
              for p in pairs]
        u = [uy[p][:CHUNK] + w2_s[rows, psl[p]] for p in pairs]
        y = [uy[p][CHUNK:] + ypar_s[rows, psl[p]] + _mm(arb_s[rows, psl[p]], _blockdiag(u[p], head0))
             for p in pairs]
        upd = [_mm_tn(jnp.concatenate([u[p], v_ref[0, rows, psl[p]]], axis=0),
                      jnp.concatenate([bh_s[rows, psl[p]], kh_s[rows, psl[p]]], axis=0)) for p in pairs]
        for p in pairs:
            y_scr[rows, psl[p]] = y[p]
            decay = jnp.concatenate([gl_s[g_rows, psl[p]]] * (PAIR // 8), axis=0)
            s_refs[p][...] = s0[p] * decay + jnp.where(same_head, upd[p], 0.0)
        return carry

    lax.fori_loop(0, n_chunks, state_body, 0)

    y = y_scr[...]
    mu = _head_sum(y, 1.0 / HEAD)
    d = y - mu
    var = _head_sum(d * d, 1.0 / HEAD)
    yn = d * lax.rsqrt(var + LN_X_EPS) * lnw_ref[...] + lnb_ref[...]
    bonus = _head_sum(r_ref[0] * k_ref[0] * rk_ref[...], 1.0) * v_ref[0]
    y_o[0] = (yn + bonus) * g_ref[0]


def _rwkv(r, lw, k, v, kk, b, g, r_k, ln_w, ln_b, tc):
    bsz, t_len, _ = r.shape
    tok = pl.BlockSpec((1, tc, D_RWKV), lambda bb, t: (bb, t, 0))
    par = pl.BlockSpec((1, D_RWKV), lambda bb, t: (0, 0))
    return pl.pallas_call(
        _rwkv_kernel,
        grid=(bsz, t_len // tc),
        in_specs=[tok] * 7 + [par] * 3,
        out_specs=tok,
        out_shape=jax.ShapeDtypeStruct((bsz, t_len, D_RWKV), F32),
        scratch_shapes=([pltpu.VMEM((PAIR, PAIR), F32)] * N_PAIRS + [pltpu.VMEM((tc, D_RWKV), F32)]
                        + [pltpu.VMEM((tc, D_RWKV), _MXU)] * 5 + [pltpu.VMEM((tc, D_RWKV), F32)] * 2
                        + [pltpu.VMEM((tc // CHUNK * 8, D_RWKV), F32)]),
        compiler_params=pltpu.CompilerParams(
            dimension_semantics=("arbitrary", "arbitrary"), vmem_limit_bytes=VMEM_LIMIT),
        name="rwkv",
    )(r, lw, k, v, kk, b, g, r_k.reshape(1, D_RWKV), ln_w.reshape(1, D_RWKV), ln_b.reshape(1, D_RWKV))


def _s5_kernel(u_ref, are_ref, aim_ref, bbd_ref, cbd_ref, dskip_ref, wglu_ref, bglu_ref, beta_ref,
               y_o, sre_ref, sim_ref, bu_ref):
    i = pl.program_id(0)
    lc, bsz, _ = u_ref.shape

    @pl.when(i == 0)
    def _():
        sre_ref[...] = jnp.zeros_like(sre_ref)
        sim_ref[...] = jnp.zeros_like(sim_ref)

    u = u_ref[...].reshape(lc * bsz, D_SSM)
    bu_ref[...] = _mm(u, bbd_ref[...])
    a_re = jnp.broadcast_to(are_ref[...], (bsz, N_STATE))
    a_im = jnp.broadcast_to(aim_ref[...], (bsz, N_STATE))

    def step(t, carry):
        s_re, s_im = carry
        rows = pl.ds(pl.multiple_of(t * bsz, bsz), bsz)
        n_re = a_re * s_re - a_im * s_im + bu_ref[rows, :N_STATE]
        n_im = a_re * s_im + a_im * s_re + bu_ref[rows, N_STATE:]
        bu_ref[rows, :N_STATE] = n_re
        bu_ref[rows, N_STATE:] = n_im
        return n_re, n_im

    s_re, s_im = lax.fori_loop(0, lc, step, (sre_ref[...], sim_ref[...]))
    sre_ref[...] = s_re
    sim_ref[...] = s_im

    y = _mm(bu_ref[...], cbd_ref[...]) + dskip_ref[...] * u
    y = jax.nn.gelu(y)
    y = y * _sigmoid(_mm(y, wglu_ref[...]) + bglu_ref[...])
    y_o[...] = _rms(y, beta_ref[...]).reshape(lc, bsz, D_SSM)


def _s5(u_tm, a_re, a_im, bb_re, bb_im, c_re, c_im, d_skip, w_glu, b_glu, beta, lc):
    t_len, bsz, _ = u_tm.shape
    eye = jnp.eye(N_GROUPS, dtype=F32)
    bbd = jnp.concatenate(
        [jnp.einsum('gpm,gh->gmhp', bb_re, eye).reshape(D_SSM, N_STATE),
         jnp.einsum('gpm,gh->gmhp', bb_im, eye).reshape(D_SSM, N_STATE)], axis=1).astype(_MXU)
    cbd = jnp.concatenate(
        [jnp.einsum('gmp,gh->gphm', c_re, eye).reshape(N_STATE, D_SSM),
         jnp.einsum('gmp,gh->gphm', -c_im, eye).reshape(N_STATE, D_SSM)], axis=0).astype(_MXU)
    full = lambda shape: pl.BlockSpec(shape, lambda i: (0,) * len(shape))
    tok = pl.BlockSpec((lc, bsz, D_SSM), lambda i: (i, 0, 0))
    return pl.pallas_call(
        _s5_kernel,
        grid=(t_len // lc,),
        in_specs=[tok, full((1, N_STATE)), full((1, N_STATE)), full((D_SSM, 2 * N_STATE)),
                  full((2 * N_STATE, D_SSM)), full((1, D_SSM)), full((D_SSM, D_SSM)),
                  full((1, D_SSM)), full((1, D_SSM))],
        out_specs=tok,
        out_shape=jax.ShapeDtypeStruct((t_len, bsz, D_SSM), F32),
        scratch_shapes=[pltpu.VMEM((bsz, N_STATE), F32), pltpu.VMEM((bsz, N_STATE), F32),
                        pltpu.VMEM((lc * bsz, 2 * N_STATE), F32)],
        compiler_params=pltpu.CompilerParams(
            dimension_semantics=("arbitrary",), vmem_limit_bytes=VMEM_LIMIT),
        name="s5",
    )(u_tm, a_re.reshape(1, N_STATE), a_im.reshape(1, N_STATE), bbd, cbd, d_skip.reshape(1, D_SSM),
      w_glu.astype(_MXU), b_glu.reshape(1, D_SSM), beta.reshape(1, D_SSM))


ROUTE_IDX = 0
ROUTE_COL = TOP_K
ROUTE_GATE = 2 * TOP_K
ROW_ALIGN = 8
BLK_BASE, BLK_CNT, BLK_OFF = 0, 1, 2


def _outproj_kernel(x_ref, yr_ref, ys_ref, wout_ref, g2_ref, rw_ref, rb_ref,
                    x2_o, h2_o, route_o, blk_o, tot_o, carry_ref):
    first = jnp.logical_and(pl.program_id(0) == 0, pl.program_id(1) == 0)

    @pl.when(first)
    def _():
        carry_ref[...] = jnp.zeros_like(carry_ref)

    x2 = (x_ref[0] + _mm(yr_ref[0], wout_ref[:D_RWKV, :]) + _mm(ys_ref[0], wout_ref[D_RWKV:, :]))
    x2_o[0] = x2
    h2 = _rms(x2, g2_ref[...])
    h2_o[0] = h2.astype(h2_o.dtype)
    logits = _mm_hi(h2, rw_ref[...]) + rb_ref[...]
    tt = logits.shape[0]
    lane = lax.broadcasted_iota(jnp.int32, (tt, LANES), 1).astype(F32)

    vals, idxs = [], []
    member = jnp.zeros((tt, LANES), F32)
    for _ in range(TOP_K):
        m = jnp.max(logits, axis=-1, keepdims=True)
        idx = jnp.min(jnp.where(logits == m, lane, float(LANES)), axis=-1, keepdims=True)
        sel = lane == idx
        member = jnp.where(sel, 1.0, member)
        logits = jnp.where(sel, -jnp.inf, logits)
        vals.append(m)
        idxs.append(idx)
    exps = [jnp.exp(v - vals[0]) for v in vals]
    denom = exps[0] + exps[1] + exps[2] + exps[3]

    before = (lax.broadcasted_iota(jnp.int32, (tt, tt), 1)
              < lax.broadcasted_iota(jnp.int32, (tt, tt), 0))
    rank_local = _mm(jnp.where(before, 1.0, 0.0), member)
    cnt = jnp.sum(member, axis=0, keepdims=True)
    cnt_al = jnp.floor((cnt + (ROW_ALIGN - 1.0)) * (1.0 / ROW_ALIGN)) * ROW_ALIGN
    upper = (lax.broadcasted_iota(jnp.int32, (LANES, LANES), 0)
             < lax.broadcasted_iota(jnp.int32, (LANES, LANES), 1))
    run_off = _mm_f32_lhs(jnp.broadcast_to(cnt_al, (8, LANES)), jnp.where(upper, 1.0, 0.0))[0:1, :]
    col_all = rank_local + run_off
    base = carry_ref[...]
    carry_ref[...] = base + cnt_al
    tot_o[...] = jnp.broadcast_to(base + cnt_al, tot_o.shape)
    rows8 = lax.broadcasted_iota(jnp.int32, (8, LANES), 0)
    blk_o[0] = jnp.where(rows8 == BLK_BASE, base,
                         jnp.where(rows8 == BLK_CNT, cnt_al, jnp.where(rows8 == BLK_OFF, run_off, 0.0)))

    route = jnp.zeros((tt, LANES), F32)
    for k in range(TOP_K):
        col_k = jnp.sum(jnp.where(lane == idxs[k], col_all, 0.0), axis=-1, keepdims=True)
        route = jnp.where(lane == ROUTE_IDX + k, idxs[k], route)
        route = jnp.where(lane == ROUTE_COL + k, col_k, route)
        route = jnp.where(lane == ROUTE_GATE + k, exps[k] / denom, route)
    route_o[0] = route


def _outproj(x, y_rwkv, y_ssm, w_out, norm2_g, router_w, router_b, tt):
    bsz, t_len, d = x.shape
    nt = t_len // tt
    rw = jnp.zeros((d, LANES), F32).at[:, :N_EXPERTS].set(router_w)
    rb = jnp.full((1, LANES), -1e30, F32).at[0, :N_EXPERTS].set(router_b)
    full = lambda shape: pl.BlockSpec(shape, lambda b, t: (0,) * len(shape))
    tok = lambda w: pl.BlockSpec((1, tt, w), lambda b, t: (b, t, 0))
    return pl.pallas_call(
        _outproj_kernel,
        grid=(bsz, nt),
        in_specs=[tok(d), tok(D_RWKV), tok(D_SSM), full((d, d)), full((1, d)), full((d, LANES)),
                  full((1, LANES))],
        out_specs=[tok(d), tok(d), tok(LANES), pl.BlockSpec((1, 8, LANES), lambda b, t: (b * nt + t, 0, 0)),
                   full((8, LANES))],
        out_shape=[jax.ShapeDtypeStruct((bsz, t_len, d), F32), jax.ShapeDtypeStruct((bsz, t_len, d), _MXU),
                   jax.ShapeDtypeStruct((bsz, t_len, LANES), F32),
                   jax.ShapeDtypeStruct((bsz * nt, 8, LANES), F32), jax.ShapeDtypeStruct((8, LANES), F32)],
        scratch_shapes=[pltpu.VMEM((1, LANES), F32)],
        compiler_params=pltpu.CompilerParams(
            dimension_semantics=("arbitrary", "arbitrary"), vmem_limit_bytes=VMEM_LIMIT),
        name="outproj_router",
    )(x, y_rwkv, y_ssm, w_out.astype(_MXU), norm2_g.reshape(1, d), rw, rb)


def _zero_tile_copy(z_ref, xs_ref, sem, start):
    return pltpu.make_async_copy(z_ref, xs_ref.at[pl.ds(start, FFN_TILE)], sem)


def _run_sizes(tt):
    sizes, s = [], ROW_ALIGN
    while s <= tt:
        sizes.append(s)
        s *= 2
    return sizes[::-1]


def _for_each_run_piece(blk, tt, start_ref, cnt_ref, off_ref, fn):
    def per_expert(e, carry):
        j = blk * N_EXPERTS + e
        cnt, off, start = cnt_ref[j], off_ref[j], start_ref[j]
        done = 0
        for size in _run_sizes(tt):
            @pl.when((cnt & size) != 0)
            def _(done=done, size=size):
                fn(pl.multiple_of(off + done, ROW_ALIGN), pl.multiple_of(start + done, ROW_ALIGN), size)
            done = done + (cnt & size)
        return carry

    lax.fori_loop(0, N_EXPERTS, per_expert, 0)


def _dispatch_kernel(start_ref, cnt_ref, off_ref, ends_ref, padded_ref, route_ref, h_ref, xs_o,
                     x_buf, z_ref, sems, z_sem):
    i = pl.program_id(0)
    n_blk = pl.num_programs(0)
    tt = h_ref.shape[0]
    slot = i % 2

    def piece(buf_slot, wait):
        def fn(buf_row, sorted_row, size):
            cp = pltpu.make_async_copy(x_buf.at[buf_slot, pl.ds(0 if wait else buf_row, size)],
                                       xs_o.at[pl.ds(0 if wait else sorted_row, size)], sems.at[buf_slot])
            cp.wait() if wait else cp.start()
        return fn

    @pl.when(i == 0)
    def _():
        z_ref[...] = jnp.zeros_like(z_ref)
        for e in range(N_EXPERTS):
            @pl.when(padded_ref[e] > 0)
            def _():
                start = pl.multiple_of(ends_ref[e] - FFN_TILE, FFN_TILE)
                _zero_tile_copy(z_ref, xs_o, z_sem, start).start()
        n_used = ends_ref[N_EXPERTS - 1] // FFN_TILE
        n_tiles = xs_o.shape[0] // FFN_TILE

        def zero_tail(j, carry):
            _zero_tile_copy(z_ref, xs_o, z_sem, pl.multiple_of(j * FFN_TILE, FFN_TILE)).start()
            return carry

        def wait_tail(j, carry):
            _zero_tile_copy(z_ref, xs_o, z_sem, 0).wait()
            return carry

        lax.fori_loop(n_used, n_tiles, zero_tail, 0)
        for e in range(N_EXPERTS):
            @pl.when(padded_ref[e] > 0)
            def _():
                _zero_tile_copy(z_ref, xs_o, z_sem, 0).wait()
        lax.fori_loop(n_used, n_tiles, wait_tail, 0)

    n_buf = x_buf.shape[1]
    route_t = route_ref[...].T
    rows = lax.broadcasted_iota(jnp.int32, (n_buf, tt), 0).astype(F32)
    onehot = jnp.zeros((n_buf, tt), F32)
    for k in range(TOP_K):
        onehot = jnp.where(rows == route_t[ROUTE_COL + k:ROUTE_COL + k + 1, :], 1.0, onehot)
    x_buf[slot] = _mm(onehot, h_ref[...])

    @pl.when(i > 0)
    def _():
        _for_each_run_piece(i - 1, tt, start_ref, cnt_ref, off_ref, piece(1 - slot, True))

    _for_each_run_piece(i, tt, start_ref, cnt_ref, off_ref, piece(slot, False))

    @pl.when(i == n_blk - 1)
    def _():
        _for_each_run_piece(i, tt, start_ref, cnt_ref, off_ref, piece(slot, True))


def _block_buffer_rows(tt):
    return TOP_K * tt + N_EXPERTS * ROW_ALIGN


def _dispatch(start, cnt, off, ends, padded, route, h2, n_rows, tt):
    n, d = h2.shape
    return pl.pallas_call(
        _dispatch_kernel,
        grid_spec=pltpu.PrefetchScalarGridSpec(
            num_scalar_prefetch=5,
            grid=(n // tt,),
            in_specs=[pl.BlockSpec((tt, LANES), lambda i, *_: (i, 0)),
                      pl.BlockSpec((tt, d), lambda i, *_: (i, 0))],
            out_specs=pl.BlockSpec(memory_space=pl.ANY),
            scratch_shapes=[pltpu.VMEM((2, _block_buffer_rows(tt), d), F32), pltpu.VMEM((FFN_TILE, d), F32),
                            pltpu.SemaphoreType.DMA((2,)), pltpu.SemaphoreType.DMA]),
        out_shape=jax.ShapeDtypeStruct((n_rows, d), F32),
        compiler_params=pltpu.CompilerParams(
            dimension_semantics=("arbitrary",), vmem_limit_bytes=VMEM_LIMIT),
        name="moe_dispatch",
    )(start, cnt, off, ends, padded, route, h2)


def _w1_perm_kernel(w_ref, o_ref):
    r = lax.broadcasted_iota(jnp.int32, (MXU_TILE, MXU_TILE), 0)
    c = lax.broadcasted_iota(jnp.int32, (MXU_TILE, MXU_TILE), 1)
    half = MXU_TILE // 2
    src = jnp.where(c < half, 2 * c, 2 * (c - half) + 1)
    perm = jnp.where(r == src, 1.0, 0.0).astype(_MXU)
    for j in range(w_ref.shape[2] // MXU_TILE):
        sl = slice(j * MXU_TILE, (j + 1) * MXU_TILE)
        o_ref[0, :, sl] = jnp.dot(w_ref[0, :, sl].astype(_MXU), perm,
                                  preferred_element_type=F32).astype(o_ref.dtype)


def _w1_perm(w1):
    e, d, f2 = w1.shape
    spec = pl.BlockSpec((1, d, f2), lambda i: (i, 0, 0))
    return pl.pallas_call(
        _w1_perm_kernel,
        grid=(e,),
        in_specs=[spec],
        out_specs=spec,
        out_shape=jax.ShapeDtypeStruct(w1.shape, _MXU),
        compiler_params=pltpu.CompilerParams(
            dimension_semantics=("arbitrary",), vmem_limit_bytes=VMEM_LIMIT),
        name="w1_regroup",
    )(w1)


def _ffn_kernel(te_ref, nused_ref, xs_ref, w1_ref, b1g_ref, b1l_ref, w2_ref, b2_ref, ys_o):
    i = pl.program_id(0)

    @pl.when(i < nused_ref[0])
    def _():
        hu = _mm(xs_ref[...], w1_ref[0])
        half = MXU_TILE // 2
        n_grp = hu.shape[1] // MXU_TILE
        h_glu = jnp.concatenate([hu[:, j * MXU_TILE:j * MXU_TILE + half] for j in range(n_grp)], axis=1)
        h_lin = jnp.concatenate([hu[:, j * MXU_TILE + half:(j + 1) * MXU_TILE] for j in range(n_grp)], axis=1)
        x_glu = jnp.minimum(h_glu + b1g_ref[0], SWIGLU_LIMIT)
        x_lin = jnp.clip(h_lin + b1l_ref[0], -SWIGLU_LIMIT, SWIGLU_LIMIT)
        act = x_glu * _sigmoid(SWIGLU_ALPHA * x_glu) * (x_lin + 1.0)
        ys_o[...] = _mm(act, w2_ref[0]) + b2_ref[0]

    @pl.when(i >= nused_ref[0])
    def _():
        ys_o[...] = jnp.zeros_like(ys_o)


def _ffn(tile_expert, n_used, xs, w1p, b1g, b1l, w2, b2):
    n_rows, d = xs.shape
    n_tiles = n_rows // FFN_TILE

    def row_map(i, te, nu):
        return (jnp.minimum(i, nu[0] - 1), 0)

    def exp_map(i, te, nu):
        return (te[i], 0, 0)

    return pl.pallas_call(
        _ffn_kernel,
        grid_spec=pltpu.PrefetchScalarGridSpec(
            num_scalar_prefetch=2,
            grid=(n_tiles,),
            in_specs=[pl.BlockSpec((FFN_TILE, d), row_map),
                      pl.BlockSpec((1, d, 2 * D_FF), exp_map),
                      pl.BlockSpec((1, 1, D_FF), exp_map), pl.BlockSpec((1, 1, D_FF), exp_map),
                      pl.BlockSpec((1, D_FF, d), exp_map), pl.BlockSpec((1, 1, d), exp_map)],
            out_specs=pl.BlockSpec((FFN_TILE, d), lambda i, te, nu: (i, 0))),
        out_shape=jax.ShapeDtypeStruct((n_rows, d), F32),
        compiler_params=pltpu.CompilerParams(
            dimension_semantics=("arbitrary",), vmem_limit_bytes=VMEM_LIMIT),
        name="moe_ffn",
    )(tile_expert, n_used, xs, w1p, b1g, b1l, w2, b2)


def _combine_kernel(start_ref, cnt_ref, off_ref, route_ref, x2_ref, ys_hbm, fg_ref, out_o, y_buf, sems):
    i = pl.program_id(0)
    n_blk = pl.num_programs(0)
    tt = x2_ref.shape[0]
    slot = i % 2

    def piece(buf_slot, wait):
        def fn(buf_row, sorted_row, size):
            cp = pltpu.make_async_copy(ys_hbm.at[pl.ds(0 if wait else sorted_row, size)],
                                       y_buf.at[buf_slot, pl.ds(0 if wait else buf_row, size)],
                                       sems.at[buf_slot])
            cp.wait() if wait else cp.start()
        return fn

    @pl.when(i == 0)
    def _():
        y_buf[...] = jnp.zeros_like(y_buf)
        _for_each_run_piece(0, tt, start_ref, cnt_ref, off_ref, piece(0, False))

    @pl.when(i + 1 < n_blk)
    def _():
        _for_each_run_piece(i + 1, tt, start_ref, cnt_ref, off_ref, piece(1 - slot, False))

    _for_each_run_piece(i, tt, start_ref, cnt_ref, off_ref, piece(slot, True))

    n_buf = y_buf.shape[1]
    route = route_ref[...]
    cols = lax.broadcasted_iota(jnp.int32, (tt, n_buf), 1).astype(F32)
    gates = jnp.zeros((tt, n_buf), F32)
    for k in range(TOP_K):
        gates = jnp.where(cols == route[:, ROUTE_COL + k:ROUTE_COL + k + 1],
                          route[:, ROUTE_GATE + k:ROUTE_GATE + k + 1], gates)
    acc = x2_ref[...] + _mm(gates, y_buf[slot])
    out_o[...] = _rms(acc, fg_ref[...])


def _combine(start, cnt, off, route, x2, ys, final_g, tt):
    n, d = x2.shape
    return pl.pallas_call(
        _combine_kernel,
        grid_spec=pltpu.PrefetchScalarGridSpec(
            num_scalar_prefetch=3,
            grid=(n // tt,),
            in_specs=[pl.BlockSpec((tt, LANES), lambda i, *_: (i, 0)),
                      pl.BlockSpec((tt, d), lambda i, *_: (i, 0)),
                      pl.BlockSpec(memory_space=pl.ANY), pl.BlockSpec((1, d), lambda i, *_: (0, 0))],
            out_specs=pl.BlockSpec((tt, d), lambda i, *_: (i, 0)),
            scratch_shapes=[pltpu.VMEM((2, _block_buffer_rows(tt), d), F32), pltpu.SemaphoreType.DMA((2,))]),
        out_shape=jax.ShapeDtypeStruct((n, d), F32),
        compiler_params=pltpu.CompilerParams(
            dimension_semantics=("arbitrary",), vmem_limit_bytes=VMEM_LIMIT),
        name="moe_combine",
    )(start, cnt, off, route, x2, ys, final_g.reshape(1, d))


def _moe(x2, h2, route, blk, totals, w1, b1, w2, b2, final_g, tt):
    n, d = x2.shape
    n_blk = n // tt
    tot = totals.astype(jnp.int32)
    padded = (tot + FFN_TILE - 1) // FFN_TILE * FFN_TILE
    ends = jnp.cumsum(padded)
    offs = ends - padded
    blk = blk[:, :, :N_EXPERTS].astype(jnp.int32)
    start = (offs[None, :] + blk[:, BLK_BASE]).reshape(-1)
    cnt = blk[:, BLK_CNT].reshape(-1)
    off = blk[:, BLK_OFF].reshape(-1)
    n_rows = -(-(n * TOP_K + n_blk * N_EXPERTS * ROW_ALIGN + N_EXPERTS * FFN_TILE) // FFN_TILE) * FFN_TILE
    n_tiles = n_rows // FFN_TILE
    tile_start = jnp.arange(n_tiles, dtype=jnp.int32) * FFN_TILE
    tile_expert = jnp.minimum(jnp.sum((tile_start[:, None] >= ends[None, :]).astype(jnp.int32), axis=1),
                              N_EXPERTS - 1)
    n_used = ends[-1:] // FFN_TILE

    xs = _dispatch(start, cnt, off, ends, padded, route, h2, n_rows, tt)
    ys = _ffn(tile_expert, n_used, xs, _w1_perm(w1), b1[:, None, 0::2], b1[:, None, 1::2],
              w2.astype(_MXU), b2[:, None, :])
    return _combine(start, cnt, off, route, x2, ys, final_g, tt)


def _tile(t_len, want):
    return want if t_len % want == 0 else t_len


def kernel(x, norm1_g, w_in, mu_shift, w0, w_up, a0, a_up, g_up, k_k, k_a, r_k, ln_x_w, ln_x_b,
           lambda_re, lambda_im, log_step, b_re, b_im, c_re, c_im, d_skip, w_glu, b_glu, beta_ssm,
           w_out, norm2_g, router_w, router_b, w1, b1, w2, b2, final_g):
    bsz, t_len, d = x.shape
    assert d == D_MODEL and t_len % CHUNK == 0 and norm1_g.shape[0] == 1
    tt = _tile(t_len, 256)
    a_re, a_im, bb_re, bb_im = _s5_params(lambda_re[0], lambda_im[0], log_step[0], b_re[0], b_im[0])
    r, lw, k, v, kk, b, g, u = _inproj(x, norm1_g[0], w_in[0], mu_shift[0], w0[0], w_up[0], a0[0],
                                       a_up[0], g_up[0], k_k[0], k_a[0], tt)
    y_rwkv = _rwkv(r, lw, k, v, kk, b, g, r_k[0], ln_x_w[0], ln_x_b[0], _tile(t_len, 512))
    y_ssm = _s5(jnp.swapaxes(u, 0, 1), a_re, a_im, bb_re, bb_im, c_re[0], c_im[0], d_skip[0],
                w_glu[0], b_glu[0], beta_ssm[0], _tile(t_len, 64))
    y_ssm = jnp.swapaxes(y_ssm, 0, 1)
    tb = _tile(t_len, 512)
    x2, h2, route, blk, totals = _outproj(x, y_rwkv, y_ssm, w_out[0], norm2_g[0], router_w[0],
                                          router_b[0], tb)
    n = bsz * t_len
    out = _moe(x2.reshape(n, d), h2.reshape(n, d), route.reshape(n, LANES), blk, totals[0, :N_EXPERTS],
               w1[0], b1[0], w2[0], b2[0], final_g, tb)
    return out.reshape(bsz, t_len, d)
```

```python
import functools

import jax
import jax.numpy as jnp
from jax import lax
from jax.experimental import pallas as pl
from jax.experimental.pallas import tpu as pltpu

F32 = jnp.float32
BF16 = jnp.bfloat16
_MXU = jnp.bfloat16

D_MODEL = 1024
D_RWKV = 512
D_SSM = 512
HEAD = 64
N_HEADS = D_RWKV // HEAD
LORA_W = 64
LORA_A = 64
LORA_G = 128
N_SHIFT = 3 * D_RWKV + LORA_W + LORA_A + LORA_G
D_IN = N_SHIFT + D_SSM
SSM_GROUP = 16
N_GROUPS = D_SSM // SSM_GROUP
SSM_STATE = 64
N_STATE = N_GROUPS * SSM_STATE
N_EXPERTS = 32
TOP_K = 4
D_FF = D_MODEL
SWIGLU_ALPHA = 1.702
SWIGLU_LIMIT = 7.0
NORM_EPS = 1e-5
LN_X_EPS = 64e-5

LANES = 128
CHUNK = 64
PAIR = 2 * HEAD
N_PAIRS = D_RWKV // PAIR
FFN_TILE = 512
VMEM_LIMIT = 56 * 1024 * 1024


def _mm(a, b):
    return jnp.dot(a.astype(_MXU), b.astype(_MXU), preferred_element_type=F32)


def _mm_nt(a, b):
    return lax.dot_general(a.astype(_MXU), b.astype(_MXU), (((1,), (1,)), ((), ())),
                           preferred_element_type=F32)


def _mm_tn(a, b):
    return lax.dot_general(a.astype(_MXU), b.astype(_MXU), (((0,), (0,)), ((), ())),
                           preferred_element_type=F32)


def _split3(a):
    h1 = a.astype(_MXU)
    r = a - h1.astype(F32)
    h2 = r.astype(_MXU)
    r = r - h2.astype(F32)
    return h1, h2, r.astype(_MXU)


def _mm_f32_lhs(a, m):
    mb = m.astype(_MXU)
    h1, h2, h3 = _split3(a)
    out = jnp.dot(h1, mb, preferred_element_type=F32)
    out = out + jnp.dot(h2, mb, preferred_element_type=F32)
    return out + jnp.dot(h3, mb, preferred_element_type=F32)


def _mm_f32_rhs(m, b):
    mb = m.astype(_MXU)
    h1, h2, h3 = _split3(b)
    out = jnp.dot(mb, h1, preferred_element_type=F32)
    out = out + jnp.dot(mb, h2, preferred_element_type=F32)
    return out + jnp.dot(mb, h3, preferred_element_type=F32)


def _mm_hi(a, b):
    a1 = a.astype(_MXU)
    a2 = (a - a1.astype(F32)).astype(_MXU)
    b1 = b.astype(_MXU)
    b2 = (b - b1.astype(F32)).astype(_MXU)
    out = jnp.dot(a1, b1, preferred_element_type=F32)
    out = out + jnp.dot(a1, b2, preferred_element_type=F32)
    return out + jnp.dot(a2, b1, preferred_element_type=F32)


def _sigmoid(z):
    return 1.0 / (1.0 + jnp.exp(-z))


def _rms(x, g):
    ms = jnp.mean(x * x, axis=-1, keepdims=True)
    return x * lax.rsqrt(ms + NORM_EPS) * g


MXU_TILE = 256


def _head_sum(x, scale):
    r = lax.broadcasted_iota(jnp.int32, (MXU_TILE, MXU_TILE), 0) // HEAD
    c = lax.broadcasted_iota(jnp.int32, (MXU_TILE, MXU_TILE), 1) // HEAD
    m = jnp.where(r == c, scale, 0.0).astype(_MXU)
    hi = x.astype(_MXU)
    lo = (x - hi.astype(F32)).astype(_MXU)
    parts = []
    for j in range(x.shape[1] // MXU_TILE):
        sl = slice(j * MXU_TILE, (j + 1) * MXU_TILE)
        parts.append(jnp.dot(hi[:, sl], m, preferred_element_type=F32)
                     + jnp.dot(lo[:, sl], m, preferred_element_type=F32))
    return jnp.concatenate(parts, axis=1)


def _s5_params_kernel(lre_ref, lim_ref, step_ref, bre_ref, bim_ref,
                      are_o, aim_o, bbre_o, bbim_o):
    lam_re = jnp.minimum(lre_ref[...], -1e-4)
    lam_im = lim_ref[...]
    dt = jnp.exp(step_ref[...])
    mag = jnp.exp(lam_re * dt)
    lb_re = mag * jnp.cos(lam_im * dt)
    lb_im = mag * jnp.sin(lam_im * dt)
    den = lam_re * lam_re + lam_im * lam_im
    num_re = lb_re - 1.0
    z_re = (num_re * lam_re + lb_im * lam_im) / den
    z_im = (lb_im * lam_re - num_re * lam_im) / den
    are_o[...] = lb_re
    aim_o[...] = lb_im
    pm = SSM_STATE * SSM_GROUP
    rep = (lax.broadcasted_iota(jnp.int32, (SSM_STATE, pm), 0)
           == lax.broadcasted_iota(jnp.int32, (SSM_STATE, pm), 1) // SSM_GROUP)
    rep = jnp.where(rep, 1.0, 0.0).astype(F32)
    zr = _mm_f32_lhs(z_re, rep)
    zi = _mm_f32_lhs(z_im, rep)
    b_re = bre_ref[...]
    b_im = bim_ref[...]
    bbre_o[...] = zr * b_re - zi * b_im
    bbim_o[...] = zr * b_im + zi * b_re


def _s5_params(lambda_re, lambda_im, log_step, b_re, b_im):
    g, p, m = b_re.shape
    outs = pl.pallas_call(
        _s5_params_kernel,
        out_shape=(jax.ShapeDtypeStruct((g, p), F32), jax.ShapeDtypeStruct((g, p), F32),
                   jax.ShapeDtypeStruct((g, p * m), F32), jax.ShapeDtypeStruct((g, p * m), F32)),
        name="s5_params",
    )(lambda_re, lambda_im, log_step.reshape(g, 1), b_re.reshape(g, p * m), b_im.reshape(g, p * m))
    a_re, a_im, bb_re, bb_im = outs
    return a_re, a_im, bb_re.reshape(g, p, m), bb_im.reshape(g, p, m)


def _inproj_kernel(x_ref, g1_ref, win_ref, mu_ref, w0_ref, wup_ref, a0_ref, aup_ref, gup_ref,
                   kk_ref, ka_ref,
                   r_o, lw_o, k_o, v_o, kk_o, b_o, g_o, u_o, carry_ref):
    ti = pl.program_id(1)
    h = _rms(x_ref[0], g1_ref[...])
    proj = _mm(h, win_ref[...])
    u_o[0] = proj[:, N_SHIFT:]
    p = proj[:, :N_SHIFT]
    tt = p.shape[0]
    carry = jnp.where(ti == 0, 0.0, carry_ref[0:1, :])
    row = lax.broadcasted_iota(jnp.int32, p.shape, 0)
    prev = jnp.where(row == 0, carry, pltpu.roll(p, 1, 0))
    carry_ref[0:1, :] = p[tt - 1:tt, :]
    ps = p + mu_ref[...] * (prev - p)

    r = ps[:, 0:D_RWKV]
    k = ps[:, D_RWKV:2 * D_RWKV]
    v = ps[:, 2 * D_RWKV:3 * D_RWKV]
    lora = ps[:, 3 * D_RWKV:3 * D_RWKV + LORA_W + LORA_A]
    g_lr = ps[:, 3 * D_RWKV + LORA_W + LORA_A:]

    z = w0_ref[...] + _mm(jnp.tanh(lora), wup_ref[...])
    nz = -z
    softplus = jnp.maximum(nz, 0.0) + jnp.log(1.0 + jnp.exp(-jnp.abs(nz)))
    log_w = -softplus - 0.5
    lw = -jnp.exp(log_w)
    a = _sigmoid(a0_ref[...] + _mm(lora, aup_ref[...]))
    g = _mm(_sigmoid(g_lr), gup_ref[...])

    kk = k * kk_ref[...]
    kk = kk / jnp.maximum(jnp.sqrt(_head_sum(kk * kk, 1.0)), 1e-12)
    k2 = k * (1.0 + (a - 1.0) * ka_ref[...])

    r_o[0] = r.astype(r_o.dtype)
    lw_o[0] = lw
    k_o[0] = k2.astype(k_o.dtype)
    v_o[0] = v.astype(v_o.dtype)
    kk_o[0] = kk.astype(kk_o.dtype)
    b_o[0] = (kk * a).astype(b_o.dtype)
    g_o[0] = g.astype(g_o.dtype)


def _inproj(x, norm1_g, w_in, mu_shift, w0, w_up, a0, a_up, g_up, k_k, k_a, tt):
    bsz, t_len, d = x.shape
    zeros_w = jnp.zeros((LORA_A, D_RWKV), F32)
    zeros_a = jnp.zeros((LORA_W, D_RWKV), F32)
    wup_pad = jnp.concatenate([w_up, zeros_w], axis=0).astype(_MXU)
    aup_pad = jnp.concatenate([zeros_a, a_up], axis=0).astype(_MXU)
    row = lambda v: v.reshape(1, -1)
    full = lambda shape: pl.BlockSpec(shape, lambda b, t: (0,) * len(shape))
    tok = lambda w: pl.BlockSpec((1, tt, w), lambda b, t: (b, t, 0))
    sds = lambda dt: jax.ShapeDtypeStruct((bsz, t_len, D_RWKV), dt)
    out_dtypes = [_MXU, F32, _MXU, _MXU, _MXU, _MXU, _MXU, F32]
    return pl.pallas_call(
        _inproj_kernel,
        grid=(bsz, t_len // tt),
        in_specs=[tok(d), full((1, d)), full((d, D_IN)), full((1, N_SHIFT)), full((1, D_RWKV)),
                  full((LORA_W + LORA_A, D_RWKV)), full((1, D_RWKV)), full((LORA_W + LORA_A, D_RWKV)),
                  full((LORA_G, D_RWKV)), full((1, D_RWKV)), full((1, D_RWKV))],
        out_specs=[tok(D_RWKV)] * 8,
        out_shape=[sds(dt) for dt in out_dtypes],
        scratch_shapes=[pltpu.VMEM((8, N_SHIFT), F32)],
        compiler_params=pltpu.CompilerParams(
            dimension_semantics=("arbitrary", "arbitrary"), vmem_limit_bytes=VMEM_LIMIT),
        name="inproj",
    )(x, row(norm1_g), w_in.astype(_MXU), row(mu_shift), row(w0), wup_pad, row(a0), aup_pad,
      g_up.astype(_MXU), row(k_k), row(k_a))


def _blockdiag(m, head0):
    return jnp.concatenate([jnp.where(head0, m, 0.0), jnp.where(head0, 0.0, m)], axis=0)


def _rwkv_kernel(r_ref, lw_ref, k_ref, v_ref, kk_ref, b_ref, g_ref, rk_ref, lnw_ref, lnb_ref,
                 y_o, s0_ref, s1_ref, s2_ref, s3_ref, y_scr,
                 w1_s, rt_s, arb_s, bh_s, kh_s, w2_s, ypar_s, gl_s):
    tb = pl.program_id(1)
    tc = r_ref.shape[1]
    n_chunks = tc // CHUNK

    @pl.when(tb == 0)
    def _():
        for s_ref in (s0_ref, s1_ref, s2_ref, s3_ref):
            s_ref[...] = jnp.zeros_like(s_ref)

    row = lax.broadcasted_iota(jnp.int32, (CHUNK, PAIR), 0)
    lane = lax.broadcasted_iota(jnp.int32, (CHUNK, PAIR), 1)
    head0 = lane < HEAD
    src = lane % HEAD
    strict = src < row
    incl = src <= row
    eye_pair = jnp.where(src == row, 1.0, 0.0).astype(F32)
    row2 = lax.broadcasted_iota(jnp.int32, (PAIR, PAIR), 0)
    lane2 = lax.broadcasted_iota(jnp.int32, (PAIR, PAIR), 1)
    same_head = (row2 < HEAD) == (lane2 < HEAD)
    tri = (lax.broadcasted_iota(jnp.int32, (CHUNK, CHUNK), 1)
           <= lax.broadcasted_iota(jnp.int32, (CHUNK, CHUNK), 0))
    tri = jnp.where(tri, 1.0, 0.0).astype(F32)

    pairs = range(N_PAIRS)
    psl = [slice(p * PAIR, (p + 1) * PAIR) for p in pairs]

    def prep_body(c, carry):
        rows = pl.ds(pl.multiple_of(c * CHUNK, CHUNK), CHUNK)
        lw = lw_ref[0, rows, :]
        cs = _mm_f32_rhs(tri, lw)
        cs_last = cs[CHUNK - 1:CHUNK, :]
        e_pos = jnp.exp(cs)
        e_neg = jnp.exp(-cs)
        e_end = jnp.exp(cs_last - cs)
        r = r_ref[0, rows, :].astype(F32)
        k = k_ref[0, rows, :].astype(F32)
        v = v_ref[0, rows, :].astype(F32)
        b = b_ref[0, rows, :].astype(F32)
        r_t = r * e_pos
        a_t = -kk_ref[0, rows, :].astype(F32) * jnp.exp(cs - lw)
        b_t = b * e_neg
        k_t = k * e_neg
        rt_s[rows, :] = r_t.astype(rt_s.dtype)
        bh_s[rows, :] = (b * e_end).astype(bh_s.dtype)
        kh_s[rows, :] = (k * e_end).astype(kh_s.dtype)
        g_rows = pl.ds(pl.multiple_of(c * 8, 8), 8)
        gl_s[g_rows, :] = jnp.broadcast_to(jnp.exp(cs_last), (8, D_RWKV))

        gram = [_mm_nt(jnp.concatenate([a_t[:, s], r_t[:, s]], axis=0),
                       jnp.concatenate([_blockdiag(b_t[:, s], head0),
                                        _blockdiag(k_t[:, s], head0)], axis=0)) for s in psl]
        a_ab = [jnp.where(strict, gm[:CHUNK, :PAIR], 0.0) for gm in gram]
        a_ak = [jnp.where(strict, gm[:CHUNK, PAIR:], 0.0) for gm in gram]
        a_rb = [jnp.where(incl, gm[CHUNK:, :PAIR], 0.0) for gm in gram]
        a_rk = [jnp.where(incl, gm[CHUNK:, PAIR:], 0.0) for gm in gram]
        for p in pairs:
            arb_s[rows, psl[p]] = a_rb[p].astype(arb_s.dtype)
        av = [_mm(jnp.concatenate([a_ak[p], a_rk[p]], axis=0), _blockdiag(v[:, psl[p]], head0))
              for p in pairs]
        for p in pairs:
            ypar_s[rows, psl[p]] = av[p][CHUNK:]

        t_inv = [eye_pair + a_ab[p] for p in pairs]
        m_pow = [_mm(a_ab[p], _blockdiag(a_ab[p], head0)) for p in pairs]
        n_levels = CHUNK.bit_length() - 2
        for level in range(n_levels):
            if level + 1 < n_levels:
                out = [_mm(m_pow[p], jnp.concatenate(
                    [_blockdiag(t_inv[p], head0), _blockdiag(m_pow[p], head0)], axis=1)) for p in pairs]
                t_inv = [t_inv[p] + out[p][:, :PAIR] for p in pairs]
                m_pow = [out[p][:, PAIR:] for p in pairs]
            else:
                t_inv = [t_inv[p] + _mm(m_pow[p], _blockdiag(t_inv[p], head0)) for p in pairs]
        w12 = [_mm(t_inv[p], jnp.concatenate(
            [_blockdiag(a_t[:, psl[p]], head0), _blockdiag(av[p][:CHUNK], head0)], axis=1)) for p in pairs]
        for p in pairs:
            w1_s[rows, psl[p]] = w12[p][:, :PAIR].astype(w1_s.dtype)
            w2_s[rows, psl[p]] = w12[p][:, PAIR:]
        return carry

    lax.fori_loop(0, n_chunks, prep_body, 0, unroll=2)

    s_refs = (s0_ref, s1_ref, s2_ref, s3_ref)

    def state_body(c, carry):
        rows = pl.ds(pl.multiple_of(c * CHUNK, CHUNK), CHUNK)
        g_rows = pl.ds(pl.multiple_of(c * 8, 8), 8)
        s0 = [s_refs[p][...] for p in pairs]
        uy = [_mm_nt(jnp.concatenate([w1_s[rows, psl[p]], rt_s[rows, psl[p]]], axis=0), s0[p])
              for p in pairs]
        u = [uy[p][:CHUNK] + w2_s[rows, psl[p]] for p in pairs]
        y = [uy[p][CHUNK:] + ypar_s[rows, psl[p]] + _mm(arb_s[rows, psl[p]], _blockdiag(u[p], head0))
             for p in pairs]
        upd = [_mm_tn(jnp.concatenate([u[p], v_ref[0, rows, psl[p]]], axis=0),
                      jnp.concatenate([bh_s[rows, psl[p]], kh_s[rows, psl[p]]], axis=0)) for p in pairs]
        for p in pairs:
            y_scr[rows, psl[p]] = y[p]
            decay = jnp.concatenate([gl_s[g_rows, psl[p]]] * (PAIR // 8), axis=0)
            s_refs[p][...] = s0[p] * decay + jnp.where(same_head, upd[p], 0.0)
        return carry

    lax.fori_loop(0, n_chunks, state_body, 0)

    y = y_scr[...]
    mu = _head_sum(y, 1.0 / HEAD)
    d = y - mu
    var = _head_sum(d * d, 1.0 / HEAD)
    yn = d * lax.rsqrt(var + LN_X_EPS) * lnw_ref[...] + lnb_ref[...]
    rk = r_ref[0].astype(F32) * k_ref[0].astype(F32) * rk_ref[...]
    bonus = _head_sum(rk, 1.0) * v_ref[0].astype(F32)
    y_o[0] = ((yn + bonus) * g_ref[0].astype(F32)).astype(y_o.dtype)


def _rwkv(r, lw, k, v, kk, b, g, r_k, ln_w, ln_b, tc):
    bsz, t_len, _ = r.shape
    tok = pl.BlockSpec((1, tc, D_RWKV), lambda bb, t: (bb, t, 0))
    par = pl.BlockSpec((1, D_RWKV), lambda bb, t: (0, 0))
    return pl.pallas_call(
        _rwkv_kernel,
        grid=(bsz, t_len // tc),
        in_specs=[tok] * 7 + [par] * 3,
        out_specs=tok,
        out_shape=jax.ShapeDtypeStruct((bsz, t_len, D_RWKV), _MXU),
        scratch_shapes=([pltpu.VMEM((PAIR, PAIR), F32)] * N_PAIRS + [pltpu.VMEM((tc, D_RWKV), F32)]
                        + [pltpu.VMEM((tc, D_RWKV), _MXU)] * 5 + [pltpu.VMEM((tc, D_RWKV), F32)] * 2
                        + [pltpu.VMEM((tc // CHUNK * 8, D_RWKV), F32)]),
        compiler_params=pltpu.CompilerParams(
            dimension_semantics=("arbitrary", "arbitrary"), vmem_limit_bytes=VMEM_LIMIT),
        name="rwkv",
    )(r, lw, k, v, kk, b, g, r_k.reshape(1, D_RWKV), ln_w.reshape(1, D_RWKV), ln_b.reshape(1, D_RWKV))


def _s5_kernel(u_ref, are_ref, aim_ref, bbd_ref, cbd_ref, dskip_ref, wglu_ref, bglu_ref, beta_ref,
               y_o, sre_ref, sim_ref, bu_ref):
    i = pl.program_id(0)
    lc, bsz, _ = u_ref.shape

    @pl.when(i == 0)
    def _():
        sre_ref[...] = jnp.zeros_like(sre_ref)
        sim_ref[...] = jnp.zeros_like(sim_ref)

    u = u_ref[...].reshape(lc * bsz, D_SSM)
    ub = u.astype(_MXU)
    chan_per_tile = MXU_TILE // SSM_STATE * SSM_GROUP
    for j in range(2 * N_STATE // MXU_TILE):
        c0 = (j * MXU_TILE % N_STATE) // MXU_TILE * chan_per_tile // MXU_TILE * MXU_TILE
        cols = slice(j * MXU_TILE, (j + 1) * MXU_TILE)
        bu_ref[:, cols] = jnp.dot(ub[:, c0:c0 + MXU_TILE], bbd_ref[c0:c0 + MXU_TILE, cols],
                                  preferred_element_type=F32)
    a_re = jnp.broadcast_to(are_ref[...], (bsz, N_STATE))
    a_im = jnp.broadcast_to(aim_ref[...], (bsz, N_STATE))

    def step(t, carry):
        s_re, s_im = carry
        rows = pl.ds(pl.multiple_of(t * bsz, bsz), bsz)
        n_re = a_re * s_re - a_im * s_im + bu_ref[rows, :N_STATE]
        n_im = a_re * s_im + a_im * s_re + bu_ref[rows, N_STATE:]
        bu_ref[rows, :N_STATE] = n_re
        bu_ref[rows, N_STATE:] = n_im
        return n_re, n_im

    s_re, s_im = lax.fori_loop(0, lc, step, (sre_ref[...], sim_ref[...]))
    sre_ref[...] = s_re
    sim_ref[...] = s_im

    state_per_tile = MXU_TILE // SSM_GROUP * SSM_STATE
    parts = []
    for n in range(D_SSM // MXU_TILE):
        cols = slice(n * MXU_TILE, (n + 1) * MXU_TILE)
        re_rows = slice(n * state_per_tile, (n + 1) * state_per_tile)
        im_rows = slice(N_STATE + n * state_per_tile, N_STATE + (n + 1) * state_per_tile)
        parts.append(_mm(bu_ref[:, re_rows], cbd_ref[re_rows, cols])
                     + _mm(bu_ref[:, im_rows], cbd_ref[im_rows, cols]))
    y = jnp.concatenate(parts, axis=1) + dskip_ref[...] * u
    y = jax.nn.gelu(y)
    y = y * _sigmoid(_mm(y, wglu_ref[...]) + bglu_ref[...])
    y_o[...] = _rms(y, beta_ref[...]).reshape(lc, bsz, D_SSM)


def _s5(u_tm, a_re, a_im, bb_re, bb_im, c_re, c_im, d_skip, w_glu, b_glu, beta, lc):
    t_len, bsz, _ = u_tm.shape
    eye = jnp.eye(N_GROUPS, dtype=F32)
    bbd = jnp.concatenate(
        [jnp.einsum('gpm,gh->gmhp', bb_re, eye).reshape(D_SSM, N_STATE),
         jnp.einsum('gpm,gh->gmhp', bb_im, eye).reshape(D_SSM, N_STATE)], axis=1).astype(_MXU)
    cbd = jnp.concatenate(
        [jnp.einsum('gmp,gh->gphm', c_re, eye).reshape(N_STATE, D_SSM),
         jnp.einsum('gmp,gh->gphm', -c_im, eye).reshape(N_STATE, D_SSM)], axis=0).astype(_MXU)
    full = lambda shape: pl.BlockSpec(shape, lambda i: (0,) * len(shape))
    tok = pl.BlockSpec((lc, bsz, D_SSM), lambda i: (i, 0, 0))
    return pl.pallas_call(
        _s5_kernel,
        grid=(t_len // lc,),
        in_specs=[tok, full((1, N_STATE)), full((1, N_STATE)), full((D_SSM, 2 * N_STATE)),
                  full((2 * N_STATE, D_SSM)), full((1, D_SSM)), full((D_SSM, D_SSM)),
                  full((1, D_SSM)), full((1, D_SSM))],
        out_specs=tok,
        out_shape=jax.ShapeDtypeStruct((t_len, bsz, D_SSM), F32),
        scratch_shapes=[pltpu.VMEM((bsz, N_STATE), F32), pltpu.VMEM((bsz, N_STATE), F32),
                        pltpu.VMEM((lc * bsz, 2 * N_STATE), F32)],
        compiler_params=pltpu.CompilerParams(
            dimension_semantics=("arbitrary",), vmem_limit_bytes=VMEM_LIMIT),
        name="s5",
    )(u_tm, a_re.reshape(1, N_STATE), a_im.reshape(1, N_STATE), bbd, cbd, d_skip.reshape(1, D_SSM),
      w_glu.astype(_MXU), b_glu.reshape(1, D_SSM), beta.reshape(1, D_SSM))


ROUTE_IDX = 0
ROUTE_COL = TOP_K
ROUTE_GATE = 2 * TOP_K
ROW_ALIGN = 8
BLK_BASE, BLK_CNT, BLK_OFF = 0, 1, 2


def _outproj_kernel(x_ref, yr_ref, ys_ref, wout_ref, g2_ref, rw_ref, rb_ref,
                    x2_o, h2_o, route_o, blk_o, tot_o, carry_ref):
    first = jnp.logical_and(pl.program_id(0) == 0, pl.program_id(1) == 0)

    @pl.when(first)
    def _():
        carry_ref[...] = jnp.zeros_like(carry_ref)

    x2 = (x_ref[0] + _mm(yr_ref[0], wout_ref[:D_RWKV, :]) + _mm(ys_ref[0], wout_ref[D_RWKV:, :]))
    x2_o[0] = x2
    h2 = _rms(x2, g2_ref[...])
    h2_o[0] = h2.astype(h2_o.dtype)
    logits = _mm_hi(h2, rw_ref[...]) + rb_ref[...]
    tt = logits.shape[0]
    lane = lax.broadcasted_iota(jnp.int32, (tt, LANES), 1).astype(F32)

    vals, idxs = [], []
    member = jnp.zeros((tt, LANES), F32)
    for _ in range(TOP_K):
        m = jnp.max(logits, axis=-1, keepdims=True)
        idx = jnp.min(jnp.where(logits == m, lane, float(LANES)), axis=-1, keepdims=True)
        sel = lane == idx
        member = jnp.where(sel, 1.0, member)
        logits = jnp.where(sel, -jnp.inf, logits)
        vals.append(m)
        idxs.append(idx)
    exps = [jnp.exp(v - vals[0]) for v in vals]
    denom = exps[0] + exps[1] + exps[2] + exps[3]

    before = (lax.broadcasted_iota(jnp.int32, (tt, tt), 1)
              < lax.broadcasted_iota(jnp.int32, (tt, tt), 0))
    rank_local = _mm(jnp.where(before, 1.0, 0.0), member)
    cnt = jnp.sum(member, axis=0, keepdims=True)
    cnt_al = jnp.floor((cnt + (ROW_ALIGN - 1.0)) * (1.0 / ROW_ALIGN)) * ROW_ALIGN
    upper = (lax.broadcasted_iota(jnp.int32, (LANES, LANES), 0)
             < lax.broadcasted_iota(jnp.int32, (LANES, LANES), 1))
    run_off = _mm_f32_lhs(jnp.broadcast_to(cnt_al, (8, LANES)), jnp.where(upper, 1.0, 0.0))[0:1, :]
    col_all = rank_local + run_off
    base = carry_ref[...]
    carry_ref[...] = base + cnt_al
    tot_o[...] = jnp.broadcast_to(base + cnt_al, tot_o.shape)
    rows8 = lax.broadcasted_iota(jnp.int32, (8, LANES), 0)
    blk_o[0] = jnp.where(rows8 == BLK_BASE, base,
                         jnp.where(rows8 == BLK_CNT, cnt_al, jnp.where(rows8 == BLK_OFF, run_off, 0.0)))

    route = jnp.zeros((tt, LANES), F32)
    for k in range(TOP_K):
        col_k = jnp.sum(jnp.where(lane == idxs[k], col_all, 0.0), axis=-1, keepdims=True)
        route = jnp.where(lane == ROUTE_IDX + k, idxs[k], route)
        route = jnp.where(lane == ROUTE_COL + k, col_k, route)
        route = jnp.where(lane == ROUTE_GATE + k, exps[k] / denom, route)
    route_o[0] = route


def _outproj(x, y_rwkv, y_ssm, w_out, norm2_g, router_w, router_b, tt):
    bsz, t_len, d = x.shape
    nt = t_len // tt
    rw = jnp.zeros((d, LANES), F32).at[:, :N_EXPERTS].set(router_w)
    rb = jnp.full((1, LANES), -1e30, F32).at[0, :N_EXPERTS].set(router_b)
    full = lambda shape: pl.BlockSpec(shape, lambda b, t: (0,) * len(shape))
    tok = lambda w: pl.BlockSpec((1, tt, w), lambda b, t: (b, t, 0))
    return pl.pallas_call(
        _outproj_kernel,
        grid=(bsz, nt),
        in_specs=[tok(d), tok(D_RWKV), tok(D_SSM), full((d, d)), full((1, d)), full((d, LANES)),
                  full((1, LANES))],
        out_specs=[tok(d), tok(d), tok(LANES), pl.BlockSpec((1, 8, LANES), lambda b, t: (b * nt + t, 0, 0)),
                   full((8, LANES))],
        out_shape=[jax.ShapeDtypeStruct((bsz, t_len, d), F32), jax.ShapeDtypeStruct((bsz, t_len, d), _MXU),
                   jax.ShapeDtypeStruct((bsz, t_len, LANES), F32),
                   jax.ShapeDtypeStruct((bsz * nt, 8, LANES), F32), jax.ShapeDtypeStruct((8, LANES), F32)],
        scratch_shapes=[pltpu.VMEM((1, LANES), F32)],
        compiler_params=pltpu.CompilerParams(
            dimension_semantics=("arbitrary", "arbitrary"), vmem_limit_bytes=VMEM_LIMIT),
        name="outproj_router",
    )(x, y_rwkv, y_ssm, w_out.astype(_MXU), norm2_g.reshape(1, d), rw, rb)


def _zero_tile_copy(z_ref, xs_ref, sem, start):
    return pltpu.make_async_copy(z_ref, xs_ref.at[pl.ds(start, FFN_TILE)], sem)


def _run_sizes(tt):
    sizes, s = [], ROW_ALIGN
    while s <= tt:
        sizes.append(s)
        s *= 2
    return sizes[::-1]


def _for_each_run_piece(blk, tt, start_ref, cnt_ref, off_ref, fn):
    def per_expert(e, carry):
        j = blk * N_EXPERTS + e
        cnt, off, start = cnt_ref[j], off_ref[j], start_ref[j]
        done = 0
        for size in _run_sizes(tt):
            @pl.when((cnt & size) != 0)
            def _(done=done, size=size):
                fn(pl.multiple_of(off + done, ROW_ALIGN), pl.multiple_of(start + done, ROW_ALIGN), size)
            done = done + (cnt & size)
        return carry

    lax.fori_loop(0, N_EXPERTS, per_expert, 0)


def _dispatch_kernel(start_ref, cnt_ref, off_ref, ends_ref, padded_ref, route_ref, h_ref, xs_o,
                     x_buf, z_ref, sems, z_sem):
    i = pl.program_id(0)
    n_blk = pl.num_programs(0)
    tt = h_ref.shape[0]
    slot = i % 2

    def piece(buf_slot, wait):
        def fn(buf_row, sorted_row, size):
            cp = pltpu.make_async_copy(x_buf.at[buf_slot, pl.ds(0 if wait else buf_row, size)],
                                       xs_o.at[pl.ds(0 if wait else sorted_row, size)], sems.at[buf_slot])
            cp.wait() if wait else cp.start()
        return fn

    @pl.when(i == 0)
    def _():
        z_ref[...] = jnp.zeros_like(z_ref)
        for e in range(N_EXPERTS):
            @pl.when(padded_ref[e] > 0)
            def _():
                start = pl.multiple_of(ends_ref[e] - FFN_TILE, FFN_TILE)
                _zero_tile_copy(z_ref, xs_o, z_sem, start).start()
        n_used = ends_ref[N_EXPERTS - 1] // FFN_TILE
        n_tiles = xs_o.shape[0] // FFN_TILE

        def zero_tail(j, carry):
            _zero_tile_copy(z_ref, xs_o, z_sem, pl.multiple_of(j * FFN_TILE, FFN_TILE)).start()
            return carry

        def wait_tail(j, carry):
            _zero_tile_copy(z_ref, xs_o, z_sem, 0).wait()
            return carry

        lax.fori_loop(n_used, n_tiles, zero_tail, 0)
        for e in range(N_EXPERTS):
            @pl.when(padded_ref[e] > 0)
            def _():
                _zero_tile_copy(z_ref, xs_o, z_sem, 0).wait()
        lax.fori_loop(n_used, n_tiles, wait_tail, 0)

    n_buf = x_buf.shape[1]
    route_t = route_ref[...].T
    rows = lax.broadcasted_iota(jnp.int32, (n_buf, tt), 0).astype(F32)
    onehot = jnp.zeros((n_buf, tt), F32)
    for k in range(TOP_K):
        onehot = jnp.where(rows == route_t[ROUTE_COL + k:ROUTE_COL + k + 1, :], 1.0, onehot)
    x_buf[slot] = _mm(onehot, h_ref[...])

    @pl.when(i > 0)
    def _():
        _for_each_run_piece(i - 1, tt, start_ref, cnt_ref, off_ref, piece(1 - slot, True))

    _for_each_run_piece(i, tt, start_ref, cnt_ref, off_ref, piece(slot, False))

    @pl.when(i == n_blk - 1)
    def _():
        _for_each_run_piece(i, tt, start_ref, cnt_ref, off_ref, piece(slot, True))


def _block_buffer_rows(tt):
    return TOP_K * tt + N_EXPERTS * ROW_ALIGN


def _dispatch(start, cnt, off, ends, padded, route, h2, n_rows, tt):
    n, d = h2.shape
    return pl.pallas_call(
        _dispatch_kernel,
        grid_spec=pltpu.PrefetchScalarGridSpec(
            num_scalar_prefetch=5,
            grid=(n // tt,),
            in_specs=[pl.BlockSpec((tt, LANES), lambda i, *_: (i, 0)),
                      pl.BlockSpec((tt, d), lambda i, *_: (i, 0))],
            out_specs=pl.BlockSpec(memory_space=pl.ANY),
            scratch_shapes=[pltpu.VMEM((2, _block_buffer_rows(tt), d), F32), pltpu.VMEM((FFN_TILE, d), F32),
                            pltpu.SemaphoreType.DMA((2,)), pltpu.SemaphoreType.DMA]),
        out_shape=jax.ShapeDtypeStruct((n_rows, d), F32),
        compiler_params=pltpu.CompilerParams(
            dimension_semantics=("arbitrary",), vmem_limit_bytes=VMEM_LIMIT),
        name="moe_dispatch",
    )(start, cnt, off, ends, padded, route, h2)


def _w1_perm_kernel(w_ref, o_ref):
    r = lax.broadcasted_iota(jnp.int32, (MXU_TILE, MXU_TILE), 0)
    c = lax.broadcasted_iota(jnp.int32, (MXU_TILE, MXU_TILE), 1)
    half = MXU_TILE // 2
    src = jnp.where(c < half, 2 * c, 2 * (c - half) + 1)
    perm = jnp.where(r == src, 1.0, 0.0).astype(_MXU)
    for j in range(w_ref.shape[2] // MXU_TILE):
        sl = slice(j * MXU_TILE, (j + 1) * MXU_TILE)
        o_ref[0, :, sl] = jnp.dot(w_ref[0, :, sl].astype(_MXU), perm,
                                  preferred_element_type=F32).astype(o_ref.dtype)


def _w1_perm(w1):
    e, d, f2 = w1.shape
    spec = pl.BlockSpec((1, d, f2), lambda i: (i, 0, 0))
    return pl.pallas_call(
        _w1_perm_kernel,
        grid=(e,),
        in_specs=[spec],
        out_specs=spec,
        out_shape=jax.ShapeDtypeStruct(w1.shape, _MXU),
        compiler_params=pltpu.CompilerParams(
            dimension_semantics=("arbitrary",), vmem_limit_bytes=VMEM_LIMIT),
        name="w1_regroup",
    )(w1)


def _ffn_kernel(te_ref, nused_ref, xs_ref, w1_ref, b1g_ref, b1l_ref, w2_ref, b2_ref, ys_o):
    i = pl.program_id(0)

    @pl.when(i < nused_ref[0])
    def _():
        hu = _mm(xs_ref[...], w1_ref[0])
        half = MXU_TILE // 2
        n_grp = hu.shape[1] // MXU_TILE
        h_glu = jnp.concatenate([hu[:, j * MXU_TILE:j * MXU_TILE + half] for j in range(n_grp)], axis=1)
        h_lin = jnp.concatenate([hu[:, j * MXU_TILE + half:(j + 1) * MXU_TILE] for j in range(n_grp)], axis=1)
        x_glu = jnp.minimum(h_glu + b1g_ref[0], SWIGLU_LIMIT)
        x_lin = jnp.clip(h_lin + b1l_ref[0], -SWIGLU_LIMIT, SWIGLU_LIMIT)
        act = x_glu * _sigmoid(SWIGLU_ALPHA * x_glu) * (x_lin + 1.0)
        ys_o[...] = _mm(act, w2_ref[0]) + b2_ref[0]

    @pl.when(i >= nused_ref[0])
    def _():
        ys_o[...] = jnp.zeros_like(ys_o)


def _ffn(tile_expert, n_used, xs, w1p, b1g, b1l, w2, b2):
    n_rows, d = xs.shape
    n_tiles = n_rows // FFN_TILE

    def row_map(i, te, nu):
        return (jnp.minimum(i, nu[0] - 1), 0)

    def exp_map(i, te, nu):
        return (te[i], 0, 0)

    return pl.pallas_call(
        _ffn_kernel,
        grid_spec=pltpu.PrefetchScalarGridSpec(
            num_scalar_prefetch=2,
            grid=(n_tiles,),
            in_specs=[pl.BlockSpec((FFN_TILE, d), row_map),
                      pl.BlockSpec((1, d, 2 * D_FF), exp_map),
                      pl.BlockSpec((1, 1, D_FF), exp_map), pl.BlockSpec((1, 1, D_FF), exp_map),
                      pl.BlockSpec((1, D_FF, d), exp_map), pl.BlockSpec((1, 1, d), exp_map)],
            out_specs=pl.BlockSpec((FFN_TILE, d), lambda i, te, nu: (i, 0))),
        out_shape=jax.ShapeDtypeStruct((n_rows, d), F32),
        compiler_params=pltpu.CompilerParams(
            dimension_semantics=("arbitrary",), vmem_limit_bytes=VMEM_LIMIT),
        name="moe_ffn",
    )(tile_expert, n_used, xs, w1p, b1g, b1l, w2, b2)


def _combine_kernel(start_ref, cnt_ref, off_ref, route_ref, x2_ref, ys_hbm, fg_ref, out_o, y_buf, sems):
    i = pl.program_id(0)
    n_blk = pl.num_programs(0)
    tt = x2_ref.shape[0]
    slot = i % 2

    def piece(buf_slot, wait):
        def fn(buf_row, sorted_row, size):
            cp = pltpu.make_async_copy(ys_hbm.at[pl.ds(0 if wait else sorted_row, size)],
                                       y_buf.at[buf_slot, pl.ds(0 if wait else buf_row, size)],
                                       sems.at[buf_slot])
            cp.wait() if wait else cp.start()
        return fn

    @pl.when(i == 0)
    def _():
        y_buf[...] = jnp.zeros_like(y_buf)
        _for_each_run_piece(0, tt, start_ref, cnt_ref, off_ref, piece(0, False))

    @pl.when(i + 1 < n_blk)
    def _():
        _for_each_run_piece(i + 1, tt, start_ref, cnt_ref, off_ref, piece(1 - slot, False))

    _for_each_run_piece(i, tt, start_ref, cnt_ref, off_ref, piece(slot, True))

    n_buf = y_buf.shape[1]
    route = route_ref[...]
    cols = lax.broadcasted_iota(jnp.int32, (tt, n_buf), 1).astype(F32)
    gates = jnp.zeros((tt, n_buf), F32)
    for k in range(TOP_K):
        gates = jnp.where(cols == route[:, ROUTE_COL + k:ROUTE_COL + k + 1],
                          route[:, ROUTE_GATE + k:ROUTE_GATE + k + 1], gates)
    acc = x2_ref[...] + _mm(gates, y_buf[slot])
    out_o[...] = _rms(acc, fg_ref[...])


def _combine(start, cnt, off, route, x2, ys, final_g, tt):
    n, d = x2.shape
    return pl.pallas_call(
        _combine_kernel,
        grid_spec=pltpu.PrefetchScalarGridSpec(
            num_scalar_prefetch=3,
            grid=(n // tt,),
            in_specs=[pl.BlockSpec((tt, LANES), lambda i, *_: (i, 0)),
                      pl.BlockSpec((tt, d), lambda i, *_: (i, 0)),
                      pl.BlockSpec(memory_space=pl.ANY), pl.BlockSpec((1, d), lambda i, *_: (0, 0))],
            out_specs=pl.BlockSpec((tt, d), lambda i, *_: (i, 0)),
            scratch_shapes=[pltpu.VMEM((2, _block_buffer_rows(tt), d), F32), pltpu.SemaphoreType.DMA((2,))]),
        out_shape=jax.ShapeDtypeStruct((n, d), F32),
        compiler_params=pltpu.CompilerParams(
            dimension_semantics=("arbitrary",), vmem_limit_bytes=VMEM_LIMIT),
        name="moe_combine",
    )(start, cnt, off, route, x2, ys, final_g.reshape(1, d))


def _moe(x2, h2, route, blk, totals, w1, b1, w2, b2, final_g, tt):
    n, d = x2.shape
    n_blk = n // tt
    tot = totals.astype(jnp.int32)
    padded = (tot + FFN_TILE - 1) // FFN_TILE * FFN_TILE
    ends = jnp.cumsum(padded)
    offs = ends - padded
    blk = blk[:, :, :N_EXPERTS].astype(jnp.int32)
    start = (offs[None, :] + blk[:, BLK_BASE]).reshape(-1)
    cnt = blk[:, BLK_CNT].reshape(-1)
    off = blk[:, BLK_OFF].reshape(-1)
    n_rows = -(-(n * TOP_K + n_blk * N_EXPERTS * ROW_ALIGN + N_EXPERTS * FFN_TILE) // FFN_TILE) * FFN_TILE
    n_tiles = n_rows // FFN_TILE
    tile_start = jnp.arange(n_tiles, dtype=jnp.int32) * FFN_TILE
    tile_expert = jnp.minimum(jnp.sum((tile_start[:, None] >= ends[None, :]).astype(jnp.int32), axis=1),
                              N_EXPERTS - 1)
    n_used = ends[-1:] // FFN_TILE

    xs = _dispatch(start, cnt, off, ends, padded, route, h2, n_rows, tt)
    ys = _ffn(tile_expert, n_used, xs, _w1_perm(w1), b1[:, None, 0::2], b1[:, None, 1::2],
              w2.astype(_MXU), b2[:, None, :])
    return _combine(start, cnt, off, route, x2, ys, final_g, tt)


def _tile(t_len, want):
    return want if t_len % want == 0 else t_len


def kernel(x, norm1_g, w_in, mu_shift, w0, w_up, a0, a_up, g_up, k_k, k_a, r_k, ln_x_w, ln_x_b,
           lambda_re, lambda_im, log_step, b_re, b_im, c_re, c_im, d_skip, w_glu, b_glu, beta_ssm,
           w_out, norm2_g, router_w, router_b, w1, b1, w2, b2, final_g):
    bsz, t_len, d = x.shape
    assert d == D_MODEL and t_len % CHUNK == 0 and norm1_g.shape[0] == 1
    tt = _tile(t_len, 256)
    a_re, a_im, bb_re, bb_im = _s5_params(lambda_re[0], lambda_im[0], log_step[0], b_re[0], b_im[0])
    r, lw, k, v, kk, b, g, u = _inproj(x, norm1_g[0], w_in[0], mu_shift[0], w0[0], w_up[0], a0[0],
                                       a_up[0], g_up[0], k_k[0], k_a[0], tt)
    y_rwkv = _rwkv(r, lw, k, v, kk, b, g, r_k[0], ln_x_w[0], ln_x_b[0], _tile(t_len, 512))
    y_ssm = _s5(jnp.swapaxes(u, 0, 1), a_re, a_im, bb_re, bb_im, c_re[0], c_im[0], d_skip[0],
                w_glu[0], b_glu[0], beta_ssm[0], _tile(t_len, 64))
    y_ssm = jnp.swapaxes(y_ssm, 0, 1)
    tb = _tile(t_len, 512)
    x2, h2, route, blk, totals = _outproj(x, y_rwkv, y_ssm, w_out[0], norm2_g[0], router_w[0],
                                          router_b[0], tb)
    n = bsz * t_len
    out = _moe(x2.reshape(n, d), h2.reshape(n, d), route.reshape(n, LANES), blk, totals[0, :N_EXPERTS],
               w1[0], b1[0], w2[0], b2[0], final_g, tb)
    return out.reshape(bsz, t_len, d)
```

```python
import functools

import jax
import jax.numpy as jnp
from jax import lax
from jax.experimental import pallas as pl
from jax.experimental.pallas import tpu as pltpu

F32 = jnp.float32
BF16 = jnp.bfloat16
_MXU = jnp.bfloat16

D_MODEL = 1024
D_RWKV = 512
D_SSM = 512
HEAD = 64
N_HEADS = D_RWKV // HEAD
LORA_W = 64
LORA_A = 64
LORA_G = 128
N_SHIFT = 3 * D_RWKV + LORA_W + LORA_A + LORA_G
D_IN = N_SHIFT + D_SSM
SSM_GROUP = 16
N_GROUPS = D_SSM // SSM_GROUP
SSM_STATE = 64
N_STATE = N_GROUPS * SSM_STATE
N_EXPERTS = 32
TOP_K = 4
D_FF = D_MODEL
SWIGLU_ALPHA = 1.702
SWIGLU_LIMIT = 7.0
NORM_EPS = 1e-5
LN_X_EPS = 64e-5

LANES = 128
CHUNK = 64
PAIR = 2 * HEAD
N_PAIRS = D_RWKV // PAIR
PREP_GROUP = 4
FFN_TILE = 512
VMEM_LIMIT = 56 * 1024 * 1024


def _mm(a, b):
    return jnp.dot(a.astype(_MXU), b.astype(_MXU), preferred_element_type=F32)


def _mm_nt(a, b):
    return lax.dot_general(a.astype(_MXU), b.astype(_MXU), (((1,), (1,)), ((), ())),
                           preferred_element_type=F32)


def _mm_tn(a, b):
    return lax.dot_general(a.astype(_MXU), b.astype(_MXU), (((0,), (0,)), ((), ())),
                           preferred_element_type=F32)


def _split3(a):
    h1 = a.astype(_MXU)
    r = a - h1.astype(F32)
    h2 = r.astype(_MXU)
    r = r - h2.astype(F32)
    return h1, h2, r.astype(_MXU)


def _mm_f32_lhs(a, m):
    mb = m.astype(_MXU)
    h1, h2, h3 = _split3(a)
    out = jnp.dot(h1, mb, preferred_element_type=F32)
    out = out + jnp.dot(h2, mb, preferred_element_type=F32)
    return out + jnp.dot(h3, mb, preferred_element_type=F32)


def _mm_f32_rhs(m, b):
    mb = m.astype(_MXU)
    h1, h2, h3 = _split3(b)
    out = jnp.dot(mb, h1, preferred_element_type=F32)
    out = out + jnp.dot(mb, h2, preferred_element_type=F32)
    return out + jnp.dot(mb, h3, preferred_element_type=F32)


def _mm_hi(a, b):
    a1 = a.astype(_MXU)
    a2 = (a - a1.astype(F32)).astype(_MXU)
    b1 = b.astype(_MXU)
    b2 = (b - b1.astype(F32)).astype(_MXU)
    out = jnp.dot(a1, b1, preferred_element_type=F32)
    out = out + jnp.dot(a1, b2, preferred_element_type=F32)
    return out + jnp.dot(a2, b1, preferred_element_type=F32)


def _sigmoid(z):
    return 1.0 / (1.0 + jnp.exp(-z))


def _rms(x, g):
    ms = jnp.mean(x * x, axis=-1, keepdims=True)
    return x * lax.rsqrt(ms + NORM_EPS) * g


MXU_TILE = 256


def _head_sum(x, scale):
    r = lax.broadcasted_iota(jnp.int32, (MXU_TILE, MXU_TILE), 0) // HEAD
    c = lax.broadcasted_iota(jnp.int32, (MXU_TILE, MXU_TILE), 1) // HEAD
    m = jnp.where(r == c, scale, 0.0).astype(_MXU)
    hi = x.astype(_MXU)
    lo = (x - hi.astype(F32)).astype(_MXU)
    parts = []
    for j in range(x.shape[1] // MXU_TILE):
        sl = slice(j * MXU_TILE, (j + 1) * MXU_TILE)
        parts.append(jnp.dot(hi[:, sl], m, preferred_element_type=F32)
                     + jnp.dot(lo[:, sl], m, preferred_element_type=F32))
    return jnp.concatenate(parts, axis=1)


def _s5_params_kernel(lre_ref, lim_ref, step_ref, bre_ref, bim_ref,
                      are_o, aim_o, bbre_o, bbim_o):
    lam_re = jnp.minimum(lre_ref[...], -1e-4)
    lam_im = lim_ref[...]
    dt = jnp.exp(step_ref[...])
    mag = jnp.exp(lam_re * dt)
    lb_re = mag * jnp.cos(lam_im * dt)
    lb_im = mag * jnp.sin(lam_im * dt)
    den = lam_re * lam_re + lam_im * lam_im
    num_re = lb_re - 1.0
    z_re = (num_re * lam_re + lb_im * lam_im) / den
    z_im = (lb_im * lam_re - num_re * lam_im) / den
    are_o[...] = lb_re
    aim_o[...] = lb_im
    pm = SSM_STATE * SSM_GROUP
    rep = (lax.broadcasted_iota(jnp.int32, (SSM_STATE, pm), 0)
           == lax.broadcasted_iota(jnp.int32, (SSM_STATE, pm), 1) // SSM_GROUP)
    rep = jnp.where(rep, 1.0, 0.0).astype(F32)
    zr = _mm_f32_lhs(z_re, rep)
    zi = _mm_f32_lhs(z_im, rep)
    b_re = bre_ref[...]
    b_im = bim_ref[...]
    bbre_o[...] = zr * b_re - zi * b_im
    bbim_o[...] = zr * b_im + zi * b_re


def _s5_params(lambda_re, lambda_im, log_step, b_re, b_im):
    g, p, m = b_re.shape
    outs = pl.pallas_call(
        _s5_params_kernel,
        out_shape=(jax.ShapeDtypeStruct((g, p), F32), jax.ShapeDtypeStruct((g, p), F32),
                   jax.ShapeDtypeStruct((g, p * m), F32), jax.ShapeDtypeStruct((g, p * m), F32)),
        name="s5_params",
    )(lambda_re, lambda_im, log_step.reshape(g, 1), b_re.reshape(g, p * m), b_im.reshape(g, p * m))
    a_re, a_im, bb_re, bb_im = outs
    return a_re, a_im, bb_re.reshape(g, p, m), bb_im.reshape(g, p, m)


def _inproj_kernel(x_ref, g1_ref, win_ref, mu_ref, w0_ref, wup_ref, a0_ref, aup_ref, gup_ref,
                   kk_ref, ka_ref,
                   r_o, lw_o, k_o, v_o, kk_o, b_o, g_o, u_o, carry_ref):
    ti = pl.program_id(1)
    h = _rms(x_ref[0], g1_ref[...])
    proj = _mm(h, win_ref[...])
    u_o[0] = proj[:, N_SHIFT:]
    p = proj[:, :N_SHIFT]
    tt = p.shape[0]
    carry = jnp.where(ti == 0, 0.0, carry_ref[0:1, :])
    row = lax.broadcasted_iota(jnp.int32, p.shape, 0)
    prev = jnp.where(row == 0, carry, pltpu.roll(p, 1, 0))
    carry_ref[0:1, :] = p[tt - 1:tt, :]
    ps = p + mu_ref[...] * (prev - p)

    r = ps[:, 0:D_RWKV]
    k = ps[:, D_RWKV:2 * D_RWKV]
    v = ps[:, 2 * D_RWKV:3 * D_RWKV]
    lora = ps[:, 3 * D_RWKV:3 * D_RWKV + LORA_W + LORA_A]
    g_lr = ps[:, 3 * D_RWKV + LORA_W + LORA_A:]

    z = w0_ref[...] + _mm(jnp.tanh(lora), wup_ref[...])
    nz = -z
    softplus = jnp.maximum(nz, 0.0) + jnp.log(1.0 + jnp.exp(-jnp.abs(nz)))
    log_w = -softplus - 0.5
    lw = -jnp.exp(log_w)
    a = _sigmoid(a0_ref[...] + _mm(lora, aup_ref[...]))
    g = _mm(_sigmoid(g_lr), gup_ref[...])

    kk = k * kk_ref[...]
    kk = kk / jnp.maximum(jnp.sqrt(_head_sum(kk * kk, 1.0)), 1e-12)
    k2 = k * (1.0 + (a - 1.0) * ka_ref[...])

    r_o[0] = r.astype(r_o.dtype)
    lw_o[0] = lw
    k_o[0] = k2.astype(k_o.dtype)
    v_o[0] = v.astype(v_o.dtype)
    kk_o[0] = kk.astype(kk_o.dtype)
    b_o[0] = (kk * a).astype(b_o.dtype)
    g_o[0] = g.astype(g_o.dtype)


def _inproj(x, norm1_g, w_in, mu_shift, w0, w_up, a0, a_up, g_up, k_k, k_a, tt):
    bsz, t_len, d = x.shape
    zeros_w = jnp.zeros((LORA_A, D_RWKV), F32)
    zeros_a = jnp.zeros((LORA_W, D_RWKV), F32)
    wup_pad = jnp.concatenate([w_up, zeros_w], axis=0).astype(_MXU)
    aup_pad = jnp.concatenate([zeros_a, a_up], axis=0).astype(_MXU)
    row = lambda v: v.reshape(1, -1)
    full = lambda shape: pl.BlockSpec(shape, lambda b, t: (0,) * len(shape))
    tok = lambda w: pl.BlockSpec((1, tt, w), lambda b, t: (b, t, 0))
    sds = lambda dt: jax.ShapeDtypeStruct((bsz, t_len, D_RWKV), dt)
    out_dtypes = [_MXU, F32, _MXU, _MXU, _MXU, _MXU, _MXU, F32]
    return pl.pallas_call(
        _inproj_kernel,
        grid=(bsz, t_len // tt),
        in_specs=[tok(d), full((1, d)), full((d, D_IN)), full((1, N_SHIFT)), full((1, D_RWKV)),
                  full((LORA_W + LORA_A, D_RWKV)), full((1, D_RWKV)), full((LORA_W + LORA_A, D_RWKV)),
                  full((LORA_G, D_RWKV)), full((1, D_RWKV)), full((1, D_RWKV))],
        out_specs=[tok(D_RWKV)] * 8,
        out_shape=[sds(dt) for dt in out_dtypes],
        scratch_shapes=[pltpu.VMEM((8, N_SHIFT), F32)],
        compiler_params=pltpu.CompilerParams(
            dimension_semantics=("arbitrary", "arbitrary"), vmem_limit_bytes=VMEM_LIMIT),
        name="inproj",
    )(x, row(norm1_g), w_in.astype(_MXU), row(mu_shift), row(w0), wup_pad, row(a0), aup_pad,
      g_up.astype(_MXU), row(k_k), row(k_a))


def _blockdiag(m, head0):
    return jnp.concatenate([jnp.where(head0, m, 0.0), jnp.where(head0, 0.0, m)], axis=0)


def _rwkv_kernel(r_ref, lw_ref, k_ref, v_ref, kk_ref, b_ref, g_ref, rk_ref, lnw_ref, lnb_ref,
                 y_o, s0_ref, s1_ref, s2_ref, s3_ref, y_scr,
                 w1_s, rt_s, arb_s, bh_s, kh_s, w2_s, ypar_s, gl_s):
    tb = pl.program_id(1)
    tc = r_ref.shape[1]
    n_chunks = tc // CHUNK

    @pl.when(tb == 0)
    def _():
        for s_ref in (s0_ref, s1_ref, s2_ref, s3_ref):
            s_ref[...] = jnp.zeros_like(s_ref)

    row = lax.broadcasted_iota(jnp.int32, (CHUNK, PAIR), 0)
    lane = lax.broadcasted_iota(jnp.int32, (CHUNK, PAIR), 1)
    head0 = lane < HEAD
    src = lane % HEAD
    strict = src < row
    incl = src <= row
    eye_pair = jnp.where(src == row, 1.0, 0.0).astype(F32)
    row2 = lax.broadcasted_iota(jnp.int32, (PAIR, PAIR), 0)
    lane2 = lax.broadcasted_iota(jnp.int32, (PAIR, PAIR), 1)
    same_head = (row2 < HEAD) == (lane2 < HEAD)
    tri = (lax.broadcasted_iota(jnp.int32, (CHUNK, CHUNK), 1)
           <= lax.broadcasted_iota(jnp.int32, (CHUNK, CHUNK), 0))
    tri = jnp.where(tri, 1.0, 0.0).astype(F32)

    pairs = range(N_PAIRS)
    psl = [slice(p * PAIR, (p + 1) * PAIR) for p in pairs]

    group = PREP_GROUP if n_chunks % PREP_GROUP == 0 else 1

    def prep_body(cg, carry):
        rows, a_t, r_t, b_t, k_t, v = [], [], [], [], [], []
        for j in range(group):
            c = cg * group + j
            rw = pl.ds(pl.multiple_of(c * CHUNK, CHUNK), CHUNK)
            lw = lw_ref[0, rw, :]
            cs = _mm_f32_rhs(tri, lw)
            cs_last = cs[CHUNK - 1:CHUNK, :]
            e_neg = jnp.exp(-cs)
            e_end = jnp.exp(cs_last - cs)
            kf = k_ref[0, rw, :].astype(F32)
            bf = b_ref[0, rw, :].astype(F32)
            rows.append(rw)
            r_t.append(r_ref[0, rw, :].astype(F32) * jnp.exp(cs))
            a_t.append(-kk_ref[0, rw, :].astype(F32) * jnp.exp(cs - lw))
            b_t.append(bf * e_neg)
            k_t.append(kf * e_neg)
            v.append(v_ref[0, rw, :].astype(F32))
            rt_s[rw, :] = r_t[j].astype(rt_s.dtype)
            bh_s[rw, :] = (bf * e_end).astype(bh_s.dtype)
            kh_s[rw, :] = (kf * e_end).astype(kh_s.dtype)
            gl_s[pl.ds(pl.multiple_of(c * 8, 8), 8), :] = jnp.broadcast_to(jnp.exp(cs_last), (8, D_RWKV))

        units = [(j, p) for j in range(group) for p in pairs]
        gram = [_mm_nt(jnp.concatenate([a_t[j][:, psl[p]], r_t[j][:, psl[p]]], axis=0),
                       jnp.concatenate([_blockdiag(b_t[j][:, psl[p]], head0),
                                        _blockdiag(k_t[j][:, psl[p]], head0)], axis=0)) for j, p in units]
        a_ab = [jnp.where(strict, gm[:CHUNK, :PAIR], 0.0) for gm in gram]
        a_ak = [jnp.where(strict, gm[:CHUNK, PAIR:], 0.0) for gm in gram]
        a_rb = [jnp.where(incl, gm[CHUNK:, :PAIR], 0.0) for gm in gram]
        a_rk = [jnp.where(incl, gm[CHUNK:, PAIR:], 0.0) for gm in gram]
        for i, (j, p) in enumerate(units):
            arb_s[rows[j], psl[p]] = a_rb[i].astype(arb_s.dtype)
        av = [_mm(jnp.concatenate([a_ak[i], a_rk[i]], axis=0), _blockdiag(v[j][:, psl[p]], head0))
              for i, (j, p) in enumerate(units)]
        for i, (j, p) in enumerate(units):
            ypar_s[rows[j], psl[p]] = av[i][CHUNK:]

        t_inv = [eye_pair + m for m in a_ab]
        m_pow = [_mm(m, _blockdiag(m, head0)) for m in a_ab]
        n_levels = CHUNK.bit_length() - 2
        for level in range(n_levels):
            if level + 1 < n_levels:
                out = [_mm(m, jnp.concatenate([_blockdiag(t, head0), _blockdiag(m, head0)], axis=1))
                       for t, m in zip(t_inv, m_pow)]
                t_inv = [t + o[:, :PAIR] for t, o in zip(t_inv, out)]
                m_pow = [o[:, PAIR:] for o in out]
            else:
                t_inv = [t + _mm(m, _blockdiag(t, head0)) for t, m in zip(t_inv, m_pow)]
        w12 = [_mm(t_inv[i], jnp.concatenate(
            [_blockdiag(a_t[j][:, psl[p]], head0), _blockdiag(av[i][:CHUNK], head0)], axis=1))
            for i, (j, p) in enumerate(units)]
        for i, (j, p) in enumerate(units):
            w1_s[rows[j], psl[p]] = w12[i][:, :PAIR].astype(w1_s.dtype)
            w2_s[rows[j], psl[p]] = w12[i][:, PAIR:]
        return carry

    lax.fori_loop(0, n_chunks // group, prep_body, 0)

    s_refs = (s0_ref, s1_ref, s2_ref, s3_ref)

    def state_body(c, carry):
        rows = pl.ds(pl.multiple_of(c * CHUNK, CHUNK), CHUNK)
        g_rows = pl.ds(pl.multiple_of(c * 8, 8), 8)
        s0 = [s_refs[p][...] for p in pairs]
        uy = [_mm_nt(jnp.concatenate([w1_s[rows, psl[p]], rt_s[rows, psl[p]]], axis=0), s0[p])
              for p in pairs]
        u = [uy[p][:CHUNK] + w2_s[rows, psl[p]] for p in pairs]
        y = [uy[p][CHUNK:] + ypar_s[rows, psl[p]] + _mm(arb_s[rows, psl[p]], _blockdiag(u[p], head0))
             for p in pairs]
        upd = [_mm_tn(jnp.concatenate([u[p], v_ref[0, rows, psl[p]]], axis=0),
                      jnp.concatenate([bh_s[rows, psl[p]], kh_s[rows, psl[p]]], axis=0)) for p in pairs]
        for p in pairs:
            y_scr[rows, psl[p]] = y[p]
            decay = jnp.concatenate([gl_s[g_rows, psl[p]]] * (PAIR // 8), axis=0)
            s_refs[p][...] = s0[p] * decay + jnp.where(same_head, upd[p], 0.0)
        return carry

    lax.fori_loop(0, n_chunks, state_body, 0)

    y = y_scr[...]
    mu = _head_sum(y, 1.0 / HEAD)
    d = y - mu
    var = _head_sum(d * d, 1.0 / HEAD)
    yn = d * lax.rsqrt(var + LN_X_EPS) * lnw_ref[...] + lnb_ref[...]
    rk = r_ref[0].astype(F32) * k_ref[0].astype(F32) * rk_ref[...]
    bonus = _head_sum(rk, 1.0) * v_ref[0].astype(F32)
    y_o[0] = ((yn + bonus) * g_ref[0].astype(F32)).astype(y_o.dtype)


def _rwkv(r, lw, k, v, kk, b, g, r_k, ln_w, ln_b, tc):
    bsz, t_len, _ = r.shape
    tok = pl.BlockSpec((1, tc, D_RWKV), lambda bb, t: (bb, t, 0))
    par = pl.BlockSpec((1, D_RWKV), lambda bb, t: (0, 0))
    return pl.pallas_call(
        _rwkv_kernel,
        grid=(bsz, t_len // tc),
        in_specs=[tok] * 7 + [par] * 3,
        out_specs=tok,
        out_shape=jax.ShapeDtypeStruct((bsz, t_len, D_RWKV), _MXU),
        scratch_shapes=([pltpu.VMEM((PAIR, PAIR), F32)] * N_PAIRS + [pltpu.VMEM((tc, D_RWKV), F32)]
                        + [pltpu.VMEM((tc, D_RWKV), _MXU)] * 5 + [pltpu.VMEM((tc, D_RWKV), F32)] * 2
                        + [pltpu.VMEM((tc // CHUNK * 8, D_RWKV), F32)]),
        compiler_params=pltpu.CompilerParams(
            dimension_semantics=("arbitrary", "arbitrary"), vmem_limit_bytes=VMEM_LIMIT),
        name="rwkv",
    )(r, lw, k, v, kk, b, g, r_k.reshape(1, D_RWKV), ln_w.reshape(1, D_RWKV), ln_b.reshape(1, D_RWKV))


def _s5_kernel(u_ref, are_ref, aim_ref, bbd_ref, cbd_ref, dskip_ref, wglu_ref, bglu_ref, beta_ref,
               y_o, sre_ref, sim_ref, bu_ref):
    i = pl.program_id(0)
    lc, bsz, _ = u_ref.shape

    @pl.when(i == 0)
    def _():
        sre_ref[...] = jnp.zeros_like(sre_ref)
        sim_ref[...] = jnp.zeros_like(sim_ref)

    u = u_ref[...].reshape(lc * bsz, D_SSM)
    ub = u.astype(_MXU)
    chan_per_tile = MXU_TILE // SSM_STATE * SSM_GROUP
    for j in range(2 * N_STATE // MXU_TILE):
        c0 = (j * MXU_TILE % N_STATE) // MXU_TILE * chan_per_tile // MXU_TILE * MXU_TILE
        cols = slice(j * MXU_TILE, (j + 1) * MXU_TILE)
        bu_ref[:, cols] = jnp.dot(ub[:, c0:c0 + MXU_TILE], bbd_ref[c0:c0 + MXU_TILE, cols],
                                  preferred_element_type=F32)
    a_re = jnp.broadcast_to(are_ref[...], (bsz, N_STATE))
    a_im = jnp.broadcast_to(aim_ref[...], (bsz, N_STATE))

    def step(t, carry):
        s_re, s_im = carry
        rows = pl.ds(pl.multiple_of(t * bsz, bsz), bsz)
        n_re = a_re * s_re - a_im * s_im + bu_ref[rows, :N_STATE]
        n_im = a_re * s_im + a_im * s_re + bu_ref[rows, N_STATE:]
        bu_ref[rows, :N_STATE] = n_re
        bu_ref[rows, N_STATE:] = n_im
        return n_re, n_im

    s_re, s_im = lax.fori_loop(0, lc, step, (sre_ref[...], sim_ref[...]))
    sre_ref[...] = s_re
    sim_ref[...] = s_im

    state_per_tile = MXU_TILE // SSM_GROUP * SSM_STATE
    parts = []
    for n in range(D_SSM // MXU_TILE):
        cols = slice(n * MXU_TILE, (n + 1) * MXU_TILE)
        re_rows = slice(n * state_per_tile, (n + 1) * state_per_tile)
        im_rows = slice(N_STATE + n * state_per_tile, N_STATE + (n + 1) * state_per_tile)
        parts.append(_mm(bu_ref[:, re_rows], cbd_ref[re_rows, cols])
                     + _mm(bu_ref[:, im_rows], cbd_ref[im_rows, cols]))
    y = jnp.concatenate(parts, axis=1) + dskip_ref[...] * u
    y = jax.nn.gelu(y)
    y = y * _sigmoid(_mm(y, wglu_ref[...]) + bglu_ref[...])
    y_o[...] = _rms(y, beta_ref[...]).reshape(lc, bsz, D_SSM)


def _s5(u_tm, a_re, a_im, bb_re, bb_im, c_re, c_im, d_skip, w_glu, b_glu, beta, lc):
    t_len, bsz, _ = u_tm.shape
    eye = jnp.eye(N_GROUPS, dtype=F32)
    bbd = jnp.concatenate(
        [jnp.einsum('gpm,gh->gmhp', bb_re, eye).reshape(D_SSM, N_STATE),
         jnp.einsum('gpm,gh->gmhp', bb_im, eye).reshape(D_SSM, N_STATE)], axis=1).astype(_MXU)
    cbd = jnp.concatenate(
        [jnp.einsum('gmp,gh->gphm', c_re, eye).reshape(N_STATE, D_SSM),
         jnp.einsum('gmp,gh->gphm', -c_im, eye).reshape(N_STATE, D_SSM)], axis=0).astype(_MXU)
    full = lambda shape: pl.BlockSpec(shape, lambda i: (0,) * len(shape))
    tok = pl.BlockSpec((lc, bsz, D_SSM), lambda i: (i, 0, 0))
    return pl.pallas_call(
        _s5_kernel,
        grid=(t_len // lc,),
        in_specs=[tok, full((1, N_STATE)), full((1, N_STATE)), full((D_SSM, 2 * N_STATE)),
                  full((2 * N_STATE, D_SSM)), full((1, D_SSM)), full((D_SSM, D_SSM)),
                  full((1, D_SSM)), full((1, D_SSM))],
        out_specs=tok,
        out_shape=jax.ShapeDtypeStruct((t_len, bsz, D_SSM), F32),
        scratch_shapes=[pltpu.VMEM((bsz, N_STATE), F32), pltpu.VMEM((bsz, N_STATE), F32),
                        pltpu.VMEM((lc * bsz, 2 * N_STATE), F32)],
        compiler_params=pltpu.CompilerParams(
            dimension_semantics=("arbitrary",), vmem_limit_bytes=VMEM_LIMIT),
        name="s5",
    )(u_tm, a_re.reshape(1, N_STATE), a_im.reshape(1, N_STATE), bbd, cbd, d_skip.reshape(1, D_SSM),
      w_glu.astype(_MXU), b_glu.reshape(1, D_SSM), beta.reshape(1, D_SSM))


ROUTE_IDX = 0
ROUTE_COL = TOP_K
ROUTE_GATE = 2 * TOP_K
ROW_ALIGN = 8
BLK_BASE, BLK_CNT, BLK_OFF = 0, 1, 2


def _outproj_kernel(x_ref, yr_ref, ys_ref, wout_ref, g2_ref, rw_ref, rb_ref,
                    x2_o, h2_o, route_o, blk_o, tot_o, carry_ref):
    first = jnp.logical_and(pl.program_id(0) == 0, pl.program_id(1) == 0)

    @pl.when(first)
    def _():
        carry_ref[...] = jnp.zeros_like(carry_ref)

    x2 = (x_ref[0] + _mm(yr_ref[0], wout_ref[:D_RWKV, :]) + _mm(ys_ref[0], wout_ref[D_RWKV:, :]))
    x2_o[0] = x2
    h2 = _rms(x2, g2_ref[...])
    h2_o[0] = h2.astype(h2_o.dtype)
    logits = _mm_hi(h2, rw_ref[...]) + rb_ref[...]
    tt = logits.shape[0]
    lane = lax.broadcasted_iota(jnp.int32, (tt, LANES), 1).astype(F32)

    vals, idxs = [], []
    member = jnp.zeros((tt, LANES), F32)
    for _ in range(TOP_K):
        m = jnp.max(logits, axis=-1, keepdims=True)
        idx = jnp.min(jnp.where(logits == m, lane, float(LANES)), axis=-1, keepdims=True)
        sel = lane == idx
        member = jnp.where(sel, 1.0, member)
        logits = jnp.where(sel, -jnp.inf, logits)
        vals.append(m)
        idxs.append(idx)
    exps = [jnp.exp(v - vals[0]) for v in vals]
    denom = exps[0] + exps[1] + exps[2] + exps[3]

    before = (lax.broadcasted_iota(jnp.int32, (tt, tt), 1)
              < lax.broadcasted_iota(jnp.int32, (tt, tt), 0))
    rank_local = _mm(jnp.where(before, 1.0, 0.0), member)
    cnt = jnp.sum(member, axis=0, keepdims=True)
    cnt_al = jnp.floor((cnt + (ROW_ALIGN - 1.0)) * (1.0 / ROW_ALIGN)) * ROW_ALIGN
    upper = (lax.broadcasted_iota(jnp.int32, (LANES, LANES), 0)
             < lax.broadcasted_iota(jnp.int32, (LANES, LANES), 1))
    run_off = _mm_f32_lhs(jnp.broadcast_to(cnt_al, (8, LANES)), jnp.where(upper, 1.0, 0.0))[0:1, :]
    col_all = rank_local + run_off
    base = carry_ref[...]
    carry_ref[...] = base + cnt_al
    tot_o[...] = jnp.broadcast_to(base + cnt_al, tot_o.shape)
    rows8 = lax.broadcasted_iota(jnp.int32, (8, LANES), 0)
    blk_o[0] = jnp.where(rows8 == BLK_BASE, base,
                         jnp.where(rows8 == BLK_CNT, cnt_al, jnp.where(rows8 == BLK_OFF, run_off, 0.0)))

    route = jnp.zeros((tt, LANES), F32)
    for k in range(TOP_K):
        col_k = jnp.sum(jnp.where(lane == idxs[k], col_all, 0.0), axis=-1, keepdims=True)
        route = jnp.where(lane == ROUTE_IDX + k, idxs[k], route)
        route = jnp.where(lane == ROUTE_COL + k, col_k, route)
        route = jnp.where(lane == ROUTE_GATE + k, exps[k] / denom, route)
    route_o[0] = route


def _outproj(x, y_rwkv, y_ssm, w_out, norm2_g, router_w, router_b, tt):
    bsz, t_len, d = x.shape
    nt = t_len // tt
    rw = jnp.zeros((d, LANES), F32).at[:, :N_EXPERTS].set(router_w)
    rb = jnp.full((1, LANES), -1e30, F32).at[0, :N_EXPERTS].set(router_b)
    full = lambda shape: pl.BlockSpec(shape, lambda b, t: (0,) * len(shape))
    tok = lambda w: pl.BlockSpec((1, tt, w), lambda b, t: (b, t, 0))
    return pl.pallas_call(
        _outproj_kernel,
        grid=(bsz, nt),
        in_specs=[tok(d), tok(D_RWKV), tok(D_SSM), full((d, d)), full((1, d)), full((d, LANES)),
                  full((1, LANES))],
        out_specs=[tok(d), tok(d), tok(LANES), pl.BlockSpec((1, 8, LANES), lambda b, t: (b * nt + t, 0, 0)),
                   full((8, LANES))],
        out_shape=[jax.ShapeDtypeStruct((bsz, t_len, d), F32), jax.ShapeDtypeStruct((bsz, t_len, d), _MXU),
                   jax.ShapeDtypeStruct((bsz, t_len, LANES), F32),
                   jax.ShapeDtypeStruct((bsz * nt, 8, LANES), F32), jax.ShapeDtypeStruct((8, LANES), F32)],
        scratch_shapes=[pltpu.VMEM((1, LANES), F32)],
        compiler_params=pltpu.CompilerParams(
            dimension_semantics=("arbitrary", "arbitrary"), vmem_limit_bytes=VMEM_LIMIT),
        name="outproj_router",
    )(x, y_rwkv, y_ssm, w_out.astype(_MXU), norm2_g.reshape(1, d), rw, rb)


def _zero_tile_copy(z_ref, xs_ref, sem, start):
    return pltpu.make_async_copy(z_ref, xs_ref.at[pl.ds(start, FFN_TILE)], sem)


def _run_sizes(tt):
    sizes, s = [], ROW_ALIGN
    while s <= tt:
        sizes.append(s)
        s *= 2
    return sizes[::-1]


def _for_each_run_piece(blk, tt, start_ref, cnt_ref, off_ref, fn):
    def per_expert(e, carry):
        j = blk * N_EXPERTS + e
        cnt, off, start = cnt_ref[j], off_ref[j], start_ref[j]
        done = 0
        for size in _run_sizes(tt):
            @pl.when((cnt & size) != 0)
            def _(done=done, size=size):
                fn(pl.multiple_of(off + done, ROW_ALIGN), pl.multiple_of(start + done, ROW_ALIGN), size)
            done = done + (cnt & size)
        return carry

    lax.fori_loop(0, N_EXPERTS, per_expert, 0)


def _dispatch_kernel(start_ref, cnt_ref, off_ref, ends_ref, padded_ref, route_ref, h_ref, xs_o,
                     x_buf, z_ref, sems, z_sem):
    i = pl.program_id(0)
    n_blk = pl.num_programs(0)
    tt = h_ref.shape[0]
    slot = i % 2

    def piece(buf_slot, wait):
        def fn(buf_row, sorted_row, size):
            cp = pltpu.make_async_copy(x_buf.at[buf_slot, pl.ds(0 if wait else buf_row, size)],
                                       xs_o.at[pl.ds(0 if wait else sorted_row, size)], sems.at[buf_slot])
            cp.wait() if wait else cp.start()
        return fn

    @pl.when(i == 0)
    def _():
        z_ref[...] = jnp.zeros_like(z_ref)
        for e in range(N_EXPERTS):
            @pl.when(padded_ref[e] > 0)
            def _():
                start = pl.multiple_of(ends_ref[e] - FFN_TILE, FFN_TILE)
                _zero_tile_copy(z_ref, xs_o, z_sem, start).start()
        n_used = ends_ref[N_EXPERTS - 1] // FFN_TILE
        n_tiles = xs_o.shape[0] // FFN_TILE

        def zero_tail(j, carry):
            _zero_tile_copy(z_ref, xs_o, z_sem, pl.multiple_of(j * FFN_TILE, FFN_TILE)).start()
            return carry

        def wait_tail(j, carry):
            _zero_tile_copy(z_ref, xs_o, z_sem, 0).wait()
            return carry

        lax.fori_loop(n_used, n_tiles, zero_tail, 0)
        for e in range(N_EXPERTS):
            @pl.when(padded_ref[e] > 0)
            def _():
                _zero_tile_copy(z_ref, xs_o, z_sem, 0).wait()
        lax.fori_loop(n_used, n_tiles, wait_tail, 0)

    n_buf = x_buf.shape[1]
    route_t = route_ref[...].T
    rows = lax.broadcasted_iota(jnp.int32, (n_buf, tt), 0).astype(F32)
    onehot = jnp.zeros((n_buf, tt), F32)
    for k in range(TOP_K):
        onehot = jnp.where(rows == route_t[ROUTE_COL + k:ROUTE_COL + k + 1, :], 1.0, onehot)
    x_buf[slot] = _mm(onehot, h_ref[...])

    @pl.when(i > 0)
    def _():
        _for_each_run_piece(i - 1, tt, start_ref, cnt_ref, off_ref, piece(1 - slot, True))

    _for_each_run_piece(i, tt, start_ref, cnt_ref, off_ref, piece(slot, False))

    @pl.when(i == n_blk - 1)
    def _():
        _for_each_run_piece(i, tt, start_ref, cnt_ref, off_ref, piece(slot, True))


def _block_buffer_rows(tt):
    return TOP_K * tt + N_EXPERTS * ROW_ALIGN


def _dispatch(start, cnt, off, ends, padded, route, h2, n_rows, tt):
    n, d = h2.shape
    return pl.pallas_call(
        _dispatch_kernel,
        grid_spec=pltpu.PrefetchScalarGridSpec(
            num_scalar_prefetch=5,
            grid=(n // tt,),
            in_specs=[pl.BlockSpec((tt, LANES), lambda i, *_: (i, 0)),
                      pl.BlockSpec((tt, d), lambda i, *_: (i, 0))],
            out_specs=pl.BlockSpec(memory_space=pl.ANY),
            scratch_shapes=[pltpu.VMEM((2, _block_buffer_rows(tt), d), F32), pltpu.VMEM((FFN_TILE, d), F32),
                            pltpu.SemaphoreType.DMA((2,)), pltpu.SemaphoreType.DMA]),
        out_shape=jax.ShapeDtypeStruct((n_rows, d), F32),
        compiler_params=pltpu.CompilerParams(
            dimension_semantics=("arbitrary",), vmem_limit_bytes=VMEM_LIMIT),
        name="moe_dispatch",
    )(start, cnt, off, ends, padded, route, h2)


def _w1_perm_kernel(w_ref, o_ref):
    r = lax.broadcasted_iota(jnp.int32, (MXU_TILE, MXU_TILE), 0)
    c = lax.broadcasted_iota(jnp.int32, (MXU_TILE, MXU_TILE), 1)
    half = MXU_TILE // 2
    src = jnp.where(c < half, 2 * c, 2 * (c - half) + 1)
    perm = jnp.where(r == src, 1.0, 0.0).astype(_MXU)
    for j in range(w_ref.shape[2] // MXU_TILE):
        sl = slice(j * MXU_TILE, (j + 1) * MXU_TILE)
        o_ref[0, :, sl] = jnp.dot(w_ref[0, :, sl].astype(_MXU), perm,
                                  preferred_element_type=F32).astype(o_ref.dtype)


def _w1_perm(w1):
    e, d, f2 = w1.shape
    spec = pl.BlockSpec((1, d, f2), lambda i: (i, 0, 0))
    return pl.pallas_call(
        _w1_perm_kernel,
        grid=(e,),
        in_specs=[spec],
        out_specs=spec,
        out_shape=jax.ShapeDtypeStruct(w1.shape, _MXU),
        compiler_params=pltpu.CompilerParams(
            dimension_semantics=("arbitrary",), vmem_limit_bytes=VMEM_LIMIT),
        name="w1_regroup",
    )(w1)


def _ffn_kernel(te_ref, nused_ref, xs_ref, w1_ref, b1g_ref, b1l_ref, w2_ref, b2_ref, ys_o):
    i = pl.program_id(0)

    @pl.when(i < nused_ref[0])
    def _():
        hu = _mm(xs_ref[...], w1_ref[0])
        half = MXU_TILE // 2
        n_grp = hu.shape[1] // MXU_TILE
        h_glu = jnp.concatenate([hu[:, j * MXU_TILE:j * MXU_TILE + half] for j in range(n_grp)], axis=1)
        h_lin = jnp.concatenate([hu[:, j * MXU_TILE + half:(j + 1) * MXU_TILE] for j in range(n_grp)], axis=1)
        x_glu = jnp.minimum(h_glu + b1g_ref[0], SWIGLU_LIMIT)
        x_lin = jnp.clip(h_lin + b1l_ref[0], -SWIGLU_LIMIT, SWIGLU_LIMIT)
        act = x_glu * _sigmoid(SWIGLU_ALPHA * x_glu) * (x_lin + 1.0)
        ys_o[...] = _mm(act, w2_ref[0]) + b2_ref[0]

    @pl.when(i >= nused_ref[0])
    def _():
        ys_o[...] = jnp.zeros_like(ys_o)


def _ffn(tile_expert, n_used, xs, w1p, b1g, b1l, w2, b2):
    n_rows, d = xs.shape
    n_tiles = n_rows // FFN_TILE

    def row_map(i, te, nu):
        return (jnp.minimum(i, nu[0] - 1), 0)

    def exp_map(i, te, nu):
        return (te[i], 0, 0)

    return pl.pallas_call(
        _ffn_kernel,
        grid_spec=pltpu.PrefetchScalarGridSpec(
            num_scalar_prefetch=2,
            grid=(n_tiles,),
            in_specs=[pl.BlockSpec((FFN_TILE, d), row_map),
                      pl.BlockSpec((1, d, 2 * D_FF), exp_map),
                      pl.BlockSpec((1, 1, D_FF), exp_map), pl.BlockSpec((1, 1, D_FF), exp_map),
                      pl.BlockSpec((1, D_FF, d), exp_map), pl.BlockSpec((1, 1, d), exp_map)],
            out_specs=pl.BlockSpec((FFN_TILE, d), lambda i, te, nu: (i, 0))),
        out_shape=jax.ShapeDtypeStruct((n_rows, d), F32),
        compiler_params=pltpu.CompilerParams(
            dimension_semantics=("arbitrary",), vmem_limit_bytes=VMEM_LIMIT),
        name="moe_ffn",
    )(tile_expert, n_used, xs, w1p, b1g, b1l, w2, b2)


def _combine_kernel(start_ref, cnt_ref, off_ref, route_ref, x2_ref, ys_hbm, fg_ref, out_o, y_buf, sems):
    i = pl.program_id(0)
    n_blk = pl.num_programs(0)
    tt = x2_ref.shape[0]
    slot = i % 2

    def piece(buf_slot, wait):
        def fn(buf_row, sorted_row, size):
            cp = pltpu.make_async_copy(ys_hbm.at[pl.ds(0 if wait else sorted_row, size)],
                                       y_buf.at[buf_slot, pl.ds(0 if wait else buf_row, size)],
                                       sems.at[buf_slot])
            cp.wait() if wait else cp.start()
        return fn

    @pl.when(i == 0)
    def _():
        y_buf[...] = jnp.zeros_like(y_buf)
        _for_each_run_piece(0, tt, start_ref, cnt_ref, off_ref, piece(0, False))

    @pl.when(i + 1 < n_blk)
    def _():
        _for_each_run_piece(i + 1, tt, start_ref, cnt_ref, off_ref, piece(1 - slot, False))

    _for_each_run_piece(i, tt, start_ref, cnt_ref, off_ref, piece(slot, True))

    n_buf = y_buf.shape[1]
    route = route_ref[...]
    cols = lax.broadcasted_iota(jnp.int32, (tt, n_buf), 1).astype(F32)
    gates = jnp.zeros((tt, n_buf), F32)
    for k in range(TOP_K):
        gates = jnp.where(cols == route[:, ROUTE_COL + k:ROUTE_COL + k + 1],
                          route[:, ROUTE_GATE + k:ROUTE_GATE + k + 1], gates)
    acc = x2_ref[...] + _mm(gates, y_buf[slot])
    out_o[...] = _rms(acc, fg_ref[...])


def _combine(start, cnt, off, route, x2, ys, final_g, tt):
    n, d = x2.shape
    return pl.pallas_call(
        _combine_kernel,
        grid_spec=pltpu.PrefetchScalarGridSpec(
            num_scalar_prefetch=3,
            grid=(n // tt,),
            in_specs=[pl.BlockSpec((tt, LANES), lambda i, *_: (i, 0)),
                      pl.BlockSpec((tt, d), lambda i, *_: (i, 0)),
                      pl.BlockSpec(memory_space=pl.ANY), pl.BlockSpec((1, d), lambda i, *_: (0, 0))],
            out_specs=pl.BlockSpec((tt, d), lambda i, *_: (i, 0)),
            scratch_shapes=[pltpu.VMEM((2, _block_buffer_rows(tt), d), F32), pltpu.SemaphoreType.DMA((2,))]),
        out_shape=jax.ShapeDtypeStruct((n, d), F32),
        compiler_params=pltpu.CompilerParams(
            dimension_semantics=("arbitrary",), vmem_limit_bytes=VMEM_LIMIT),
        name="moe_combine",
    )(start, cnt, off, route, x2, ys, final_g.reshape(1, d))


def _moe(x2, h2, route, blk, totals, w1, b1, w2, b2, final_g, tt):
    n, d = x2.shape
    n_blk = n // tt
    tot = totals.astype(jnp.int32)
    padded = (tot + FFN_TILE - 1) // FFN_TILE * FFN_TILE
    ends = jnp.cumsum(padded)
    offs = ends - padded
    blk = blk[:, :, :N_EXPERTS].astype(jnp.int32)
    start = (offs[None, :] + blk[:, BLK_BASE]).reshape(-1)
    cnt = blk[:, BLK_CNT].reshape(-1)
    off = blk[:, BLK_OFF].reshape(-1)
    n_rows = -(-(n * TOP_K + n_blk * N_EXPERTS * ROW_ALIGN + N_EXPERTS * FFN_TILE) // FFN_TILE) * FFN_TILE
    n_tiles = n_rows // FFN_TILE
    tile_start = jnp.arange(n_tiles, dtype=jnp.int32) * FFN_TILE
    tile_expert = jnp.minimum(jnp.sum((tile_start[:, None] >= ends[None, :]).astype(jnp.int32), axis=1),
                              N_EXPERTS - 1)
    n_used = ends[-1:] // FFN_TILE

    xs = _dispatch(start, cnt, off, ends, padded, route, h2, n_rows, tt)
    ys = _ffn(tile_expert, n_used, xs, _w1_perm(w1), b1[:, None, 0::2], b1[:, None, 1::2],
              w2.astype(_MXU), b2[:, None, :])
    return _combine(start, cnt, off, route, x2, ys, final_g, tt)


def _tile(t_len, want):
    return want if t_len % want == 0 else t_len


def kernel(x, norm1_g, w_in, mu_shift, w0, w_up, a0, a_up, g_up, k_k, k_a, r_k, ln_x_w, ln_x_b,
           lambda_re, lambda_im, log_step, b_re, b_im, c_re, c_im, d_skip, w_glu, b_glu, beta_ssm,
           w_out, norm2_g, router_w, router_b, w1, b1, w2, b2, final_g):
    bsz, t_len, d = x.shape
    assert d == D_MODEL and t_len % CHUNK == 0 and norm1_g.shape[0] == 1
    tt = _tile(t_len, 256)
    a_re, a_im, bb_re, bb_im = _s5_params(lambda_re[0], lambda_im[0], log_step[0], b_re[0], b_im[0])
    r, lw, k, v, kk, b, g, u = _inproj(x, norm1_g[0], w_in[0], mu_shift[0], w0[0], w_up[0], a0[0],
                                       a_up[0], g_up[0], k_k[0], k_a[0], tt)
    y_rwkv = _rwkv(r, lw, k, v, kk, b, g, r_k[0], ln_x_w[0], ln_x_b[0], _tile(t_len, 512))
    y_ssm = _s5(jnp.swapaxes(u, 0, 1), a_re, a_im, bb_re, bb_im, c_re[0], c_im[0], d_skip[0],
                w_glu[0], b_glu[0], beta_ssm[0], _tile(t_len, 64))
    y_ssm = jnp.swapaxes(y_ssm, 0, 1)
    tb = _tile(t_len, 512)
    x2, h2, route, blk, totals = _outproj(x, y_rwkv, y_ssm, w_out[0], norm2_g[0], router_w[0],
                                          router_b[0], tb)
    n = bsz * t_len
    out = _moe(x2.reshape(n, d), h2.reshape(n, d), route.reshape(n, LANES), blk, totals[0, :N_EXPERTS],
               w1[0], b1[0], w2[0], b2[0], final_g, tb)
    return out.reshape(bsz, t_len, d)
```

```python
import functools

import jax
import jax.numpy as jnp
from jax import lax
from jax.experimental import pallas as pl
from jax.experimental.pallas import tpu as pltpu

F32 = jnp.float32
BF16 = jnp.bfloat16
_MXU = jnp.bfloat16

D_MODEL = 1024
D_RWKV = 512
D_SSM = 512
HEAD = 64
N_HEADS = D_RWKV // HEAD
LORA_W = 64
LORA_A = 64
LORA_G = 128
N_SHIFT = 3 * D_RWKV + LORA_W + LORA_A + LORA_G
D_IN = N_SHIFT + D_SSM
SSM_GROUP = 16
N_GROUPS = D_SSM // SSM_GROUP
SSM_STATE = 64
N_STATE = N_GROUPS * SSM_STATE
N_EXPERTS = 32
TOP_K = 4
D_FF = D_MODEL
SWIGLU_ALPHA = 1.702
SWIGLU_LIMIT = 7.0
NORM_EPS = 1e-5
LN_X_EPS = 64e-5

LANES = 128
CHUNK = 64
PAIR = 2 * HEAD
N_PAIRS = D_RWKV // PAIR
PREP_GROUP = 4
FFN_TILE = 512
VMEM_LIMIT = 56 * 1024 * 1024


def _mm(a, b):
    return jnp.dot(a.astype(_MXU), b.astype(_MXU), preferred_element_type=F32)


def _mm_nt(a, b):
    return lax.dot_general(a.astype(_MXU), b.astype(_MXU), (((1,), (1,)), ((), ())),
                           preferred_element_type=F32)


def _mm_tn(a, b):
    return lax.dot_general(a.astype(_MXU), b.astype(_MXU), (((0,), (0,)), ((), ())),
                           preferred_element_type=F32)


def _split3(a):
    h1 = a.astype(_MXU)
    r = a - h1.astype(F32)
    h2 = r.astype(_MXU)
    r = r - h2.astype(F32)
    return h1, h2, r.astype(_MXU)


def _mm_f32_lhs(a, m):
    mb = m.astype(_MXU)
    h1, h2, h3 = _split3(a)
    out = jnp.dot(h1, mb, preferred_element_type=F32)
    out = out + jnp.dot(h2, mb, preferred_element_type=F32)
    return out + jnp.dot(h3, mb, preferred_element_type=F32)


def _mm_f32_rhs(m, b):
    mb = m.astype(_MXU)
    h1, h2, h3 = _split3(b)
    out = jnp.dot(mb, h1, preferred_element_type=F32)
    out = out + jnp.dot(mb, h2, preferred_element_type=F32)
    return out + jnp.dot(mb, h3, preferred_element_type=F32)


def _mm_hi(a, b):
    a1 = a.astype(_MXU)
    a2 = (a - a1.astype(F32)).astype(_MXU)
    b1 = b.astype(_MXU)
    b2 = (b - b1.astype(F32)).astype(_MXU)
    out = jnp.dot(a1, b1, preferred_element_type=F32)
    out = out + jnp.dot(a1, b2, preferred_element_type=F32)
    return out + jnp.dot(a2, b1, preferred_element_type=F32)


def _sigmoid(z):
    return 1.0 / (1.0 + jnp.exp(-z))


def _rms(x, g):
    ms = jnp.mean(x * x, axis=-1, keepdims=True)
    return x * lax.rsqrt(ms + NORM_EPS) * g


MXU_TILE = 256


def _head_sum(x, scale):
    r = lax.broadcasted_iota(jnp.int32, (MXU_TILE, MXU_TILE), 0) // HEAD
    c = lax.broadcasted_iota(jnp.int32, (MXU_TILE, MXU_TILE), 1) // HEAD
    m = jnp.where(r == c, scale, 0.0).astype(_MXU)
    hi = x.astype(_MXU)
    lo = (x - hi.astype(F32)).astype(_MXU)
    parts = []
    for j in range(x.shape[1] // MXU_TILE):
        sl = slice(j * MXU_TILE, (j + 1) * MXU_TILE)
        parts.append(jnp.dot(hi[:, sl], m, preferred_element_type=F32)
                     + jnp.dot(lo[:, sl], m, preferred_element_type=F32))
    return jnp.concatenate(parts, axis=1)


def _s5_params_kernel(lre_ref, lim_ref, step_ref, bre_ref, bim_ref,
                      are_o, aim_o, bbre_o, bbim_o):
    lam_re = jnp.minimum(lre_ref[...], -1e-4)
    lam_im = lim_ref[...]
    dt = jnp.exp(step_ref[...])
    mag = jnp.exp(lam_re * dt)
    lb_re = mag * jnp.cos(lam_im * dt)
    lb_im = mag * jnp.sin(lam_im * dt)
    den = lam_re * lam_re + lam_im * lam_im
    num_re = lb_re - 1.0
    z_re = (num_re * lam_re + lb_im * lam_im) / den
    z_im = (lb_im * lam_re - num_re * lam_im) / den
    are_o[...] = lb_re
    aim_o[...] = lb_im
    pm = SSM_STATE * SSM_GROUP
    rep = (lax.broadcasted_iota(jnp.int32, (SSM_STATE, pm), 0)
           == lax.broadcasted_iota(jnp.int32, (SSM_STATE, pm), 1) // SSM_GROUP)
    rep = jnp.where(rep, 1.0, 0.0).astype(F32)
    zr = _mm_f32_lhs(z_re, rep)
    zi = _mm_f32_lhs(z_im, rep)
    b_re = bre_ref[...]
    b_im = bim_ref[...]
    bbre_o[...] = zr * b_re - zi * b_im
    bbim_o[...] = zr * b_im + zi * b_re


def _s5_params(lambda_re, lambda_im, log_step, b_re, b_im):
    g, p, m = b_re.shape
    outs = pl.pallas_call(
        _s5_params_kernel,
        out_shape=(jax.ShapeDtypeStruct((g, p), F32), jax.ShapeDtypeStruct((g, p), F32),
                   jax.ShapeDtypeStruct((g, p * m), F32), jax.ShapeDtypeStruct((g, p * m), F32)),
        name="s5_params",
    )(lambda_re, lambda_im, log_step.reshape(g, 1), b_re.reshape(g, p * m), b_im.reshape(g, p * m))
    a_re, a_im, bb_re, bb_im = outs
    return a_re, a_im, bb_re.reshape(g, p, m), bb_im.reshape(g, p, m)


def _inproj_kernel(x_ref, g1_ref, win_ref, mu_ref, w0_ref, wup_ref, a0_ref, aup_ref, gup_ref,
                   kk_ref, ka_ref,
                   r_o, lw_o, k_o, v_o, kk_o, b_o, g_o, u_o, carry_ref):
    ti = pl.program_id(1)
    h = _rms(x_ref[0], g1_ref[...])
    proj = _mm(h, win_ref[...])
    u_o[0] = proj[:, N_SHIFT:]
    p = proj[:, :N_SHIFT]
    tt = p.shape[0]
    carry = jnp.where(ti == 0, 0.0, carry_ref[0:1, :])
    row = lax.broadcasted_iota(jnp.int32, p.shape, 0)
    prev = jnp.where(row == 0, carry, pltpu.roll(p, 1, 0))
    carry_ref[0:1, :] = p[tt - 1:tt, :]
    ps = p + mu_ref[...] * (prev - p)

    r = ps[:, 0:D_RWKV]
    k = ps[:, D_RWKV:2 * D_RWKV]
    v = ps[:, 2 * D_RWKV:3 * D_RWKV]
    lora = ps[:, 3 * D_RWKV:3 * D_RWKV + LORA_W + LORA_A]
    g_lr = ps[:, 3 * D_RWKV + LORA_W + LORA_A:]

    z = w0_ref[...] + _mm(jnp.tanh(lora), wup_ref[...])
    nz = -z
    softplus = jnp.maximum(nz, 0.0) + jnp.log(1.0 + jnp.exp(-jnp.abs(nz)))
    log_w = -softplus - 0.5
    lw = -jnp.exp(log_w)
    a = _sigmoid(a0_ref[...] + _mm(lora, aup_ref[...]))
    g = _mm(_sigmoid(g_lr), gup_ref[...])

    kk = k * kk_ref[...]
    kk = kk / jnp.maximum(jnp.sqrt(_head_sum(kk * kk, 1.0)), 1e-12)
    k2 = k * (1.0 + (a - 1.0) * ka_ref[...])

    r_o[0] = r.astype(r_o.dtype)
    lw_o[0] = lw
    k_o[0] = k2.astype(k_o.dtype)
    v_o[0] = v.astype(v_o.dtype)
    kk_o[0] = kk.astype(kk_o.dtype)
    b_o[0] = (kk * a).astype(b_o.dtype)
    g_o[0] = g.astype(g_o.dtype)


def _inproj(x, norm1_g, w_in, mu_shift, w0, w_up, a0, a_up, g_up, k_k, k_a, tt):
    bsz, t_len, d = x.shape
    zeros_w = jnp.zeros((LORA_A, D_RWKV), F32)
    zeros_a = jnp.zeros((LORA_W, D_RWKV), F32)
    wup_pad = jnp.concatenate([w_up, zeros_w], axis=0).astype(_MXU)
    aup_pad = jnp.concatenate([zeros_a, a_up], axis=0).astype(_MXU)
    row = lambda v: v.reshape(1, -1)
    full = lambda shape: pl.BlockSpec(shape, lambda b, t: (0,) * len(shape))
    tok = lambda w: pl.BlockSpec((1, tt, w), lambda b, t: (b, t, 0))
    sds = lambda dt: jax.ShapeDtypeStruct((bsz, t_len, D_RWKV), dt)
    out_dtypes = [_MXU, F32, _MXU, _MXU, _MXU, _MXU, _MXU, F32]
    return pl.pallas_call(
        _inproj_kernel,
        grid=(bsz, t_len // tt),
        in_specs=[tok(d), full((1, d)), full((d, D_IN)), full((1, N_SHIFT)), full((1, D_RWKV)),
                  full((LORA_W + LORA_A, D_RWKV)), full((1, D_RWKV)), full((LORA_W + LORA_A, D_RWKV)),
                  full((LORA_G, D_RWKV)), full((1, D_RWKV)), full((1, D_RWKV))],
        out_specs=[tok(D_RWKV)] * 8,
        out_shape=[sds(dt) for dt in out_dtypes],
        scratch_shapes=[pltpu.VMEM((8, N_SHIFT), F32)],
        compiler_params=pltpu.CompilerParams(
            dimension_semantics=("arbitrary", "arbitrary"), vmem_limit_bytes=VMEM_LIMIT),
        name="inproj",
    )(x, row(norm1_g), w_in.astype(_MXU), row(mu_shift), row(w0), wup_pad, row(a0), aup_pad,
      g_up.astype(_MXU), row(k_k), row(k_a))


def _blockdiag(m, head0):
    return jnp.concatenate([jnp.where(head0, m, 0.0), jnp.where(head0, 0.0, m)], axis=0)


def _rwkv_kernel(r_ref, lw_ref, k_ref, v_ref, kk_ref, b_ref, g_ref, rk_ref, lnw_ref, lnb_ref,
                 y_o, s0_ref, s1_ref, s2_ref, s3_ref, y_scr,
                 w1_s, rt_s, arb_s, bh_s, kh_s, w2_s, ypar_s, gl_s):
    tb = pl.program_id(1)
    tc = r_ref.shape[1]
    n_chunks = tc // CHUNK

    @pl.when(tb == 0)
    def _():
        for s_ref in (s0_ref, s1_ref, s2_ref, s3_ref):
            s_ref[...] = jnp.zeros_like(s_ref)

    row = lax.broadcasted_iota(jnp.int32, (CHUNK, PAIR), 0)
    lane = lax.broadcasted_iota(jnp.int32, (CHUNK, PAIR), 1)
    head0 = lane < HEAD
    src = lane % HEAD
    strict = src < row
    incl = src <= row
    eye_pair = jnp.where(src == row, 1.0, 0.0).astype(F32)
    row2 = lax.broadcasted_iota(jnp.int32, (PAIR, PAIR), 0)
    lane2 = lax.broadcasted_iota(jnp.int32, (PAIR, PAIR), 1)
    same_head = (row2 < HEAD) == (lane2 < HEAD)
    tri = (lax.broadcasted_iota(jnp.int32, (CHUNK, CHUNK), 1)
           <= lax.broadcasted_iota(jnp.int32, (CHUNK, CHUNK), 0))
    tri = jnp.where(tri, 1.0, 0.0).astype(F32)

    pairs = range(N_PAIRS)
    psl = [slice(p * PAIR, (p + 1) * PAIR) for p in pairs]

    group = PREP_GROUP if n_chunks % PREP_GROUP == 0 else 1

    def prep_body(cg, carry):
        rows, a_t, r_t, b_t, k_t, v = [], [], [], [], [], []
        for j in range(group):
            c = cg * group + j
            rw = pl.ds(pl.multiple_of(c * CHUNK, CHUNK), CHUNK)
            lw = lw_ref[0, rw, :]
            cs = _mm_f32_rhs(tri, lw)
            cs_last = cs[CHUNK - 1:CHUNK, :]
            e_neg = jnp.exp(-cs)
            e_end = jnp.exp(cs_last - cs)
            kf = k_ref[0, rw, :].astype(F32)
            bf = b_ref[0, rw, :].astype(F32)
            rows.append(rw)
            r_t.append(r_ref[0, rw, :].astype(F32) * jnp.exp(cs))
            a_t.append(-kk_ref[0, rw, :].astype(F32) * jnp.exp(cs - lw))
            b_t.append(bf * e_neg)
            k_t.append(kf * e_neg)
            v.append(v_ref[0, rw, :].astype(F32))
            rt_s[rw, :] = r_t[j].astype(rt_s.dtype)
            bh_s[rw, :] = (bf * e_end).astype(bh_s.dtype)
            kh_s[rw, :] = (kf * e_end).astype(kh_s.dtype)
            gl_s[pl.ds(pl.multiple_of(c * 8, 8), 8), :] = jnp.broadcast_to(jnp.exp(cs_last), (8, D_RWKV))

        units = [(j, p) for j in range(group) for p in pairs]
        gram = [_mm_nt(jnp.concatenate([a_t[j][:, psl[p]], r_t[j][:, psl[p]]], axis=0),
                       jnp.concatenate([_blockdiag(b_t[j][:, psl[p]], head0),
                                        _blockdiag(k_t[j][:, psl[p]], head0)], axis=0)) for j, p in units]
        a_ab = [jnp.where(strict, gm[:CHUNK, :PAIR], 0.0) for gm in gram]
        a_ak = [jnp.where(strict, gm[:CHUNK, PAIR:], 0.0) for gm in gram]
        a_rb = [jnp.where(incl, gm[CHUNK:, :PAIR], 0.0) for gm in gram]
        a_rk = [jnp.where(incl, gm[CHUNK:, PAIR:], 0.0) for gm in gram]
        for i, (j, p) in enumerate(units):
            arb_s[rows[j], psl[p]] = a_rb[i].astype(arb_s.dtype)
        av = [_mm(jnp.concatenate([a_ak[i], a_rk[i]], axis=0), _blockdiag(v[j][:, psl[p]], head0))
              for i, (j, p) in enumerate(units)]
        for i, (j, p) in enumerate(units):
            ypar_s[rows[j], psl[p]] = av[i][CHUNK:]

        t_inv = [eye_pair + m for m in a_ab]
        m_pow = [_mm(m, _blockdiag(m, head0)) for m in a_ab]
        n_levels = CHUNK.bit_length() - 2
        for level in range(n_levels):
            if level + 1 < n_levels:
                out = [_mm(m, jnp.concatenate([_blockdiag(t, head0), _blockdiag(m, head0)], axis=1))
                       for t, m in zip(t_inv, m_pow)]
                t_inv = [t + o[:, :PAIR] for t, o in zip(t_inv, out)]
                m_pow = [o[:, PAIR:] for o in out]
            else:
                t_inv = [t + _mm(m, _blockdiag(t, head0)) for t, m in zip(t_inv, m_pow)]
        w12 = [_mm(t_inv[i], jnp.concatenate(
            [_blockdiag(a_t[j][:, psl[p]], head0), _blockdiag(av[i][:CHUNK], head0)], axis=1))
            for i, (j, p) in enumerate(units)]
        for i, (j, p) in enumerate(units):
            w1_s[rows[j], psl[p]] = w12[i][:, :PAIR].astype(w1_s.dtype)
            w2_s[rows[j], psl[p]] = w12[i][:, PAIR:]
        return carry

    lax.fori_loop(0, n_chunks // group, prep_body, 0)

    s_refs = (s0_ref, s1_ref, s2_ref, s3_ref)

    def state_body(c, carry):
        rows = pl.ds(pl.multiple_of(c * CHUNK, CHUNK), CHUNK)
        g_rows = pl.ds(pl.multiple_of(c * 8, 8), 8)
        s0 = [s_refs[p][...] for p in pairs]
        uy = [_mm_nt(jnp.concatenate([w1_s[rows, psl[p]], rt_s[rows, psl[p]]], axis=0), s0[p])
              for p in pairs]
        u = [uy[p][:CHUNK] + w2_s[rows, psl[p]] for p in pairs]
        y = [uy[p][CHUNK:] + ypar_s[rows, psl[p]] + _mm(arb_s[rows, psl[p]], _blockdiag(u[p], head0))
             for p in pairs]
        upd = [_mm_tn(jnp.concatenate([u[p], v_ref[0, rows, psl[p]]], axis=0),
                      jnp.concatenate([bh_s[rows, psl[p]], kh_s[rows, psl[p]]], axis=0)) for p in pairs]
        for p in pairs:
            y_scr[rows, psl[p]] = y[p]
            decay = jnp.concatenate([gl_s[g_rows, psl[p]]] * (PAIR // 8), axis=0)
            s_refs[p][...] = s0[p] * decay + jnp.where(same_head, upd[p], 0.0)
        return carry

    lax.fori_loop(0, n_chunks, state_body, 0)

    y = y_scr[...]
    mu = _head_sum(y, 1.0 / HEAD)
    d = y - mu
    var = _head_sum(d * d, 1.0 / HEAD)
    yn = d * lax.rsqrt(var + LN_X_EPS) * lnw_ref[...] + lnb_ref[...]
    rk = r_ref[0].astype(F32) * k_ref[0].astype(F32) * rk_ref[...]
    bonus = _head_sum(rk, 1.0) * v_ref[0].astype(F32)
    y_o[0] = ((yn + bonus) * g_ref[0].astype(F32)).astype(y_o.dtype)


def _rwkv(r, lw, k, v, kk, b, g, r_k, ln_w, ln_b, tc):
    bsz, t_len, _ = r.shape
    tok = pl.BlockSpec((1, tc, D_RWKV), lambda bb, t: (bb, t, 0))
    par = pl.BlockSpec((1, D_RWKV), lambda bb, t: (0, 0))
    return pl.pallas_call(
        _rwkv_kernel,
        grid=(bsz, t_len // tc),
        in_specs=[tok] * 7 + [par] * 3,
        out_specs=tok,
        out_shape=jax.ShapeDtypeStruct((bsz, t_len, D_RWKV), _MXU),
        scratch_shapes=([pltpu.VMEM((PAIR, PAIR), F32)] * N_PAIRS + [pltpu.VMEM((tc, D_RWKV), F32)]
                        + [pltpu.VMEM((tc, D_RWKV), _MXU)] * 5 + [pltpu.VMEM((tc, D_RWKV), F32)] * 2
                        + [pltpu.VMEM((tc // CHUNK * 8, D_RWKV), F32)]),
        compiler_params=pltpu.CompilerParams(
            dimension_semantics=("arbitrary", "arbitrary"), vmem_limit_bytes=VMEM_LIMIT),
        name="rwkv",
    )(r, lw, k, v, kk, b, g, r_k.reshape(1, D_RWKV), ln_w.reshape(1, D_RWKV), ln_b.reshape(1, D_RWKV))


def _s5_kernel(u_ref, are_ref, aim_ref, bbd_ref, cbd_ref, dskip_ref, wglu_ref, bglu_ref, beta_ref,
               y_o, sre_ref, sim_ref, bu_ref):
    i = pl.program_id(0)
    lc, bsz, _ = u_ref.shape

    @pl.when(i == 0)
    def _():
        sre_ref[...] = jnp.zeros_like(sre_ref)
        sim_ref[...] = jnp.zeros_like(sim_ref)

    u = u_ref[...].reshape(lc * bsz, D_SSM)
    ub = u.astype(_MXU)
    chan_per_tile = MXU_TILE // SSM_STATE * SSM_GROUP
    for j in range(2 * N_STATE // MXU_TILE):
        c0 = (j * MXU_TILE % N_STATE) // MXU_TILE * chan_per_tile // MXU_TILE * MXU_TILE
        cols = slice(j * MXU_TILE, (j + 1) * MXU_TILE)
        bu_ref[:, cols] = jnp.dot(ub[:, c0:c0 + MXU_TILE], bbd_ref[c0:c0 + MXU_TILE, cols],
                                  preferred_element_type=F32)
    a_re = jnp.broadcast_to(are_ref[...], (bsz, N_STATE))
    a_im = jnp.broadcast_to(aim_ref[...], (bsz, N_STATE))

    def step(t, carry):
        s_re, s_im = carry
        rows = pl.ds(pl.multiple_of(t * bsz, bsz), bsz)
        n_re = a_re * s_re - a_im * s_im + bu_ref[rows, :N_STATE]
        n_im = a_re * s_im + a_im * s_re + bu_ref[rows, N_STATE:]
        bu_ref[rows, :N_STATE] = n_re
        bu_ref[rows, N_STATE:] = n_im
        return n_re, n_im

    s_re, s_im = lax.fori_loop(0, lc, step, (sre_ref[...], sim_ref[...]))
    sre_ref[...] = s_re
    sim_ref[...] = s_im

    state_per_tile = MXU_TILE // SSM_GROUP * SSM_STATE
    parts = []
    for n in range(D_SSM // MXU_TILE):
        cols = slice(n * MXU_TILE, (n + 1) * MXU_TILE)
        re_rows = slice(n * state_per_tile, (n + 1) * state_per_tile)
        im_rows = slice(N_STATE + n * state_per_tile, N_STATE + (n + 1) * state_per_tile)
        parts.append(_mm(bu_ref[:, re_rows], cbd_ref[re_rows, cols])
                     + _mm(bu_ref[:, im_rows], cbd_ref[im_rows, cols]))
    y = jnp.concatenate(parts, axis=1) + dskip_ref[...] * u
    y = jax.nn.gelu(y)
    y = y * _sigmoid(_mm(y, wglu_ref[...]) + bglu_ref[...])
    y_o[...] = _rms(y, beta_ref[...]).reshape(lc, bsz, D_SSM)


def _s5(u_tm, a_re, a_im, bb_re, bb_im, c_re, c_im, d_skip, w_glu, b_glu, beta, lc):
    t_len, bsz, _ = u_tm.shape
    eye = jnp.eye(N_GROUPS, dtype=F32)
    bbd = jnp.concatenate(
        [jnp.einsum('gpm,gh->gmhp', bb_re, eye).reshape(D_SSM, N_STATE),
         jnp.einsum('gpm,gh->gmhp', bb_im, eye).reshape(D_SSM, N_STATE)], axis=1).astype(_MXU)
    cbd = jnp.concatenate(
        [jnp.einsum('gmp,gh->gphm', c_re, eye).reshape(N_STATE, D_SSM),
         jnp.einsum('gmp,gh->gphm', -c_im, eye).reshape(N_STATE, D_SSM)], axis=0).astype(_MXU)
    full = lambda shape: pl.BlockSpec(shape, lambda i: (0,) * len(shape))
    tok = pl.BlockSpec((lc, bsz, D_SSM), lambda i: (i, 0, 0))
    return pl.pallas_call(
        _s5_kernel,
        grid=(t_len // lc,),
        in_specs=[tok, full((1, N_STATE)), full((1, N_STATE)), full((D_SSM, 2 * N_STATE)),
                  full((2 * N_STATE, D_SSM)), full((1, D_SSM)), full((D_SSM, D_SSM)),
                  full((1, D_SSM)), full((1, D_SSM))],
        out_specs=tok,
        out_shape=jax.ShapeDtypeStruct((t_len, bsz, D_SSM), F32),
        scratch_shapes=[pltpu.VMEM((bsz, N_STATE), F32), pltpu.VMEM((bsz, N_STATE), F32),
                        pltpu.VMEM((lc * bsz, 2 * N_STATE), F32)],
        compiler_params=pltpu.CompilerParams(
            dimension_semantics=("arbitrary",), vmem_limit_bytes=VMEM_LIMIT),
        name="s5",
    )(u_tm, a_re.reshape(1, N_STATE), a_im.reshape(1, N_STATE), bbd, cbd, d_skip.reshape(1, D_SSM),
      w_glu.astype(_MXU), b_glu.reshape(1, D_SSM), beta.reshape(1, D_SSM))


ROUTE_IDX = 0
ROUTE_COL = TOP_K
ROUTE_GATE = 2 * TOP_K
ROW_ALIGN = 8
BLK_BASE, BLK_CNT, BLK_OFF = 0, 1, 2


def _outproj_kernel(x_ref, yr_ref, ys_ref, wout_ref, g2_ref, rw_ref, rb_ref,
                    x2_o, h2_o, route_o, blk_o, tot_o, carry_ref):
    first = jnp.logical_and(pl.program_id(0) == 0, pl.program_id(1) == 0)

    @pl.when(first)
    def _():
        carry_ref[...] = jnp.zeros_like(carry_ref)

    x2 = (x_ref[0] + _mm(yr_ref[0], wout_ref[:D_RWKV, :]) + _mm(ys_ref[0], wout_ref[D_RWKV:, :]))
    x2_o[0] = x2
    h2 = _rms(x2, g2_ref[...])
    h2_o[0] = h2.astype(h2_o.dtype)
    logits = _mm_hi(h2, rw_ref[...]) + rb_ref[...]
    tt = logits.shape[0]
    lane = lax.broadcasted_iota(jnp.int32, (tt, LANES), 1).astype(F32)

    vals, idxs = [], []
    member = jnp.zeros((tt, LANES), F32)
    for _ in range(TOP_K):
        m = jnp.max(logits, axis=-1, keepdims=True)
        idx = jnp.min(jnp.where(logits == m, lane, float(LANES)), axis=-1, keepdims=True)
        sel = lane == idx
        member = jnp.where(sel, 1.0, member)
        logits = jnp.where(sel, -jnp.inf, logits)
        vals.append(m)
        idxs.append(idx)
    exps = [jnp.exp(v - vals[0]) for v in vals]
    denom = exps[0] + exps[1] + exps[2] + exps[3]

    before = (lax.broadcasted_iota(jnp.int32, (tt, tt), 1)
              < lax.broadcasted_iota(jnp.int32, (tt, tt), 0))
    rank_local = _mm(jnp.where(before, 1.0, 0.0), member)
    cnt = jnp.sum(member, axis=0, keepdims=True)
    cnt_al = jnp.floor((cnt + (ROW_ALIGN - 1.0)) * (1.0 / ROW_ALIGN)) * ROW_ALIGN
    upper = (lax.broadcasted_iota(jnp.int32, (LANES, LANES), 0)
             < lax.broadcasted_iota(jnp.int32, (LANES, LANES), 1))
    run_off = _mm_f32_lhs(jnp.broadcast_to(cnt_al, (8, LANES)), jnp.where(upper, 1.0, 0.0))[0:1, :]
    col_all = rank_local + run_off
    base = carry_ref[...]
    carry_ref[...] = base + cnt_al
    tot_o[...] = jnp.broadcast_to(base + cnt_al, tot_o.shape)
    rows8 = lax.broadcasted_iota(jnp.int32, (8, LANES), 0)
    blk_o[0] = jnp.where(rows8 == BLK_BASE, base,
                         jnp.where(rows8 == BLK_CNT, cnt_al, jnp.where(rows8 == BLK_OFF, run_off, 0.0)))

    route = jnp.zeros((tt, LANES), F32)
    for k in range(TOP_K):
        col_k = jnp.sum(jnp.where(lane == idxs[k], col_all, 0.0), axis=-1, keepdims=True)
        route = jnp.where(lane == ROUTE_IDX + k, idxs[k], route)
        route = jnp.where(lane == ROUTE_COL + k, col_k, route)
        route = jnp.where(lane == ROUTE_GATE + k, exps[k] / denom, route)
    route_o[0] = route


def _outproj(x, y_rwkv, y_ssm, w_out, norm2_g, router_w, router_b, tt):
    bsz, t_len, d = x.shape
    nt = t_len // tt
    rw = jnp.zeros((d, LANES), F32).at[:, :N_EXPERTS].set(router_w)
    rb = jnp.full((1, LANES), -1e30, F32).at[0, :N_EXPERTS].set(router_b)
    full = lambda shape: pl.BlockSpec(shape, lambda b, t: (0,) * len(shape))
    tok = lambda w: pl.BlockSpec((1, tt, w), lambda b, t: (b, t, 0))
    return pl.pallas_call(
        _outproj_kernel,
        grid=(bsz, nt),
        in_specs=[tok(d), tok(D_RWKV), tok(D_SSM), full((d, d)), full((1, d)), full((d, LANES)),
                  full((1, LANES))],
        out_specs=[tok(d), tok(d), tok(LANES), pl.BlockSpec((1, 8, LANES), lambda b, t: (b * nt + t, 0, 0)),
                   full((8, LANES))],
        out_shape=[jax.ShapeDtypeStruct((bsz, t_len, d), F32), jax.ShapeDtypeStruct((bsz, t_len, d), _MXU),
                   jax.ShapeDtypeStruct((bsz, t_len, LANES), F32),
                   jax.ShapeDtypeStruct((bsz * nt, 8, LANES), F32), jax.ShapeDtypeStruct((8, LANES), F32)],
        scratch_shapes=[pltpu.VMEM((1, LANES), F32)],
        compiler_params=pltpu.CompilerParams(
            dimension_semantics=("arbitrary", "arbitrary"), vmem_limit_bytes=VMEM_LIMIT),
        name="outproj_router",
    )(x, y_rwkv, y_ssm, w_out.astype(_MXU), norm2_g.reshape(1, d), rw, rb)


def _zero_tile_copy(z_ref, xs_ref, sem, start):
    return pltpu.make_async_copy(z_ref, xs_ref.at[pl.ds(start, FFN_TILE)], sem)


def _run_sizes(tt):
    sizes, s = [], ROW_ALIGN
    while s <= tt:
        sizes.append(s)
        s *= 2
    return sizes[::-1]


N_RUNS = N_EXPERTS + 1


def _for_each_run_piece(blk, tt, start_ref, cnt_ref, off_ref, fn):
    def per_expert(e, carry):
        j = blk * N_RUNS + e
        cnt, off, start = cnt_ref[j], off_ref[j], start_ref[j]
        done = 0
        for size in _run_sizes(tt):
            @pl.when((cnt & size) != 0)
            def _(done=done, size=size):
                fn(pl.multiple_of(off + done, ROW_ALIGN), pl.multiple_of(start + done, ROW_ALIGN), size)
            done = done + (cnt & size)
        return carry

    lax.fori_loop(0, N_RUNS, per_expert, 0)


def _dispatch_kernel(start_ref, cnt_ref, off_ref, ends_ref, padded_ref, route_ref, h_ref, xs_o,
                     x_buf, z_ref, sems, z_sem):
    i = pl.program_id(0)
    n_blk = pl.num_programs(0)
    tt = h_ref.shape[0]
    n_buf = x_buf.shape[1]
    slot = i % 2

    def start_pieces(buf_slot):
        def fn(buf_row, sorted_row, size):
            pltpu.make_async_copy(x_buf.at[buf_slot, pl.ds(buf_row, size)],
                                  xs_o.at[pl.ds(sorted_row, size)], sems.at[buf_slot]).start()
        return fn

    def wait_block(buf_slot):
        pltpu.make_async_copy(x_buf.at[buf_slot], xs_o.at[pl.ds(0, n_buf)], sems.at[buf_slot]).wait()

    @pl.when(i == 0)
    def _():
        z_ref[...] = jnp.zeros_like(z_ref)
        for e in range(N_EXPERTS):
            @pl.when(padded_ref[e] > 0)
            def _():
                start = pl.multiple_of(ends_ref[e] - FFN_TILE, FFN_TILE)
                _zero_tile_copy(z_ref, xs_o, z_sem, start).start()
        n_used = ends_ref[N_EXPERTS - 1] // FFN_TILE
        n_tiles = xs_o.shape[0] // FFN_TILE

        def zero_tail(j, carry):
            _zero_tile_copy(z_ref, xs_o, z_sem, pl.multiple_of(j * FFN_TILE, FFN_TILE)).start()
            return carry

        def wait_tail(j, carry):
            _zero_tile_copy(z_ref, xs_o, z_sem, 0).wait()
            return carry

        lax.fori_loop(n_used, n_tiles, zero_tail, 0)
        for e in range(N_EXPERTS):
            @pl.when(padded_ref[e] > 0)
            def _():
                _zero_tile_copy(z_ref, xs_o, z_sem, 0).wait()
        lax.fori_loop(n_used, n_tiles, wait_tail, 0)

    route_t = route_ref[...].T
    rows = lax.broadcasted_iota(jnp.int32, (n_buf, tt), 0).astype(F32)
    onehot = jnp.zeros((n_buf, tt), F32)
    for k in range(TOP_K):
        onehot = jnp.where(rows == route_t[ROUTE_COL + k:ROUTE_COL + k + 1, :], 1.0, onehot)
    x_buf[slot] = _mm(onehot, h_ref[...])

    @pl.when(i > 0)
    def _():
        wait_block(1 - slot)

    _for_each_run_piece(i, tt, start_ref, cnt_ref, off_ref, start_pieces(slot))

    @pl.when(i == n_blk - 1)
    def _():
        wait_block(slot)


def _block_buffer_rows(tt):
    return TOP_K * tt + N_EXPERTS * ROW_ALIGN


def _dispatch(start, cnt, off, ends, padded, route, h2, n_rows, tt):
    n, d = h2.shape
    return pl.pallas_call(
        _dispatch_kernel,
        grid_spec=pltpu.PrefetchScalarGridSpec(
            num_scalar_prefetch=5,
            grid=(n // tt,),
            in_specs=[pl.BlockSpec((tt, LANES), lambda i, *_: (i, 0)),
                      pl.BlockSpec((tt, d), lambda i, *_: (i, 0))],
            out_specs=pl.BlockSpec(memory_space=pl.ANY),
            scratch_shapes=[pltpu.VMEM((2, _block_buffer_rows(tt), d), F32), pltpu.VMEM((FFN_TILE, d), F32),
                            pltpu.SemaphoreType.DMA((2,)), pltpu.SemaphoreType.DMA]),
        out_shape=jax.ShapeDtypeStruct((n_rows, d), F32),
        compiler_params=pltpu.CompilerParams(
            dimension_semantics=("arbitrary",), vmem_limit_bytes=VMEM_LIMIT),
        name="moe_dispatch",
    )(start, cnt, off, ends, padded, route, h2)


def _w1_perm_kernel(w_ref, o_ref):
    r = lax.broadcasted_iota(jnp.int32, (MXU_TILE, MXU_TILE), 0)
    c = lax.broadcasted_iota(jnp.int32, (MXU_TILE, MXU_TILE), 1)
    half = MXU_TILE // 2
    src = jnp.where(c < half, 2 * c, 2 * (c - half) + 1)
    perm = jnp.where(r == src, 1.0, 0.0).astype(_MXU)
    for j in range(w_ref.shape[2] // MXU_TILE):
        sl = slice(j * MXU_TILE, (j + 1) * MXU_TILE)
        o_ref[0, :, sl] = jnp.dot(w_ref[0, :, sl].astype(_MXU), perm,
                                  preferred_element_type=F32).astype(o_ref.dtype)


def _w1_perm(w1):
    e, d, f2 = w1.shape
    spec = pl.BlockSpec((1, d, f2), lambda i: (i, 0, 0))
    return pl.pallas_call(
        _w1_perm_kernel,
        grid=(e,),
        in_specs=[spec],
        out_specs=spec,
        out_shape=jax.ShapeDtypeStruct(w1.shape, _MXU),
        compiler_params=pltpu.CompilerParams(
            dimension_semantics=("arbitrary",), vmem_limit_bytes=VMEM_LIMIT),
        name="w1_regroup",
    )(w1)


def _ffn_kernel(te_ref, nused_ref, xs_ref, w1_ref, b1g_ref, b1l_ref, w2_ref, b2_ref, ys_o):
    i = pl.program_id(0)

    @pl.when(i < nused_ref[0])
    def _():
        hu = _mm(xs_ref[...], w1_ref[0])
        half = MXU_TILE // 2
        n_grp = hu.shape[1] // MXU_TILE
        h_glu = jnp.concatenate([hu[:, j * MXU_TILE:j * MXU_TILE + half] for j in range(n_grp)], axis=1)
        h_lin = jnp.concatenate([hu[:, j * MXU_TILE + half:(j + 1) * MXU_TILE] for j in range(n_grp)], axis=1)
        x_glu = jnp.minimum(h_glu + b1g_ref[0], SWIGLU_LIMIT)
        x_lin = jnp.clip(h_lin + b1l_ref[0], -SWIGLU_LIMIT, SWIGLU_LIMIT)
        act = x_glu * _sigmoid(SWIGLU_ALPHA * x_glu) * (x_lin + 1.0)
        ys_o[...] = _mm(act, w2_ref[0]) + b2_ref[0]

    @pl.when(i >= nused_ref[0])
    def _():
        ys_o[...] = jnp.zeros_like(ys_o)


def _ffn(tile_expert, n_used, xs, w1p, b1g, b1l, w2, b2):
    n_rows, d = xs.shape
    n_tiles = n_rows // FFN_TILE

    def row_map(i, te, nu):
        return (jnp.minimum(i, nu[0] - 1), 0)

    def exp_map(i, te, nu):
        return (te[i], 0, 0)

    return pl.pallas_call(
        _ffn_kernel,
        grid_spec=pltpu.PrefetchScalarGridSpec(
            num_scalar_prefetch=2,
            grid=(n_tiles,),
            in_specs=[pl.BlockSpec((FFN_TILE, d), row_map),
                      pl.BlockSpec((1, d, 2 * D_FF), exp_map),
                      pl.BlockSpec((1, 1, D_FF), exp_map), pl.BlockSpec((1, 1, D_FF), exp_map),
                      pl.BlockSpec((1, D_FF, d), exp_map), pl.BlockSpec((1, 1, d), exp_map)],
            out_specs=pl.BlockSpec((FFN_TILE, d), lambda i, te, nu: (i, 0))),
        out_shape=jax.ShapeDtypeStruct((n_rows, d), F32),
        compiler_params=pltpu.CompilerParams(
            dimension_semantics=("arbitrary",), vmem_limit_bytes=VMEM_LIMIT),
        name="moe_ffn",
    )(tile_expert, n_used, xs, w1p, b1g, b1l, w2, b2)


def _combine_kernel(start_ref, cnt_ref, off_ref, route_ref, x2_ref, ys_hbm, fg_ref, out_o, y_buf, sems):
    i = pl.program_id(0)
    n_blk = pl.num_programs(0)
    tt = x2_ref.shape[0]
    n_buf = y_buf.shape[1]
    slot = i % 2

    def start_pieces(buf_slot):
        def fn(buf_row, sorted_row, size):
            pltpu.make_async_copy(ys_hbm.at[pl.ds(sorted_row, size)],
                                  y_buf.at[buf_slot, pl.ds(buf_row, size)], sems.at[buf_slot]).start()
        return fn

    @pl.when(i == 0)
    def _():
        _for_each_run_piece(0, tt, start_ref, cnt_ref, off_ref, start_pieces(0))

    @pl.when(i + 1 < n_blk)
    def _():
        _for_each_run_piece(i + 1, tt, start_ref, cnt_ref, off_ref, start_pieces(1 - slot))

    pltpu.make_async_copy(ys_hbm.at[pl.ds(0, n_buf)], y_buf.at[slot], sems.at[slot]).wait()

    route = route_ref[...]
    cols = lax.broadcasted_iota(jnp.int32, (tt, n_buf), 1).astype(F32)
    gates = jnp.zeros((tt, n_buf), F32)
    for k in range(TOP_K):
        gates = jnp.where(cols == route[:, ROUTE_COL + k:ROUTE_COL + k + 1],
                          route[:, ROUTE_GATE + k:ROUTE_GATE + k + 1], gates)
    acc = x2_ref[...] + _mm(gates, y_buf[slot])
    out_o[...] = _rms(acc, fg_ref[...])


def _combine(start, cnt, off, route, x2, ys, final_g, tt):
    n, d = x2.shape
    return pl.pallas_call(
        _combine_kernel,
        grid_spec=pltpu.PrefetchScalarGridSpec(
            num_scalar_prefetch=3,
            grid=(n // tt,),
            in_specs=[pl.BlockSpec((tt, LANES), lambda i, *_: (i, 0)),
                      pl.BlockSpec((tt, d), lambda i, *_: (i, 0)),
                      pl.BlockSpec(memory_space=pl.ANY), pl.BlockSpec((1, d), lambda i, *_: (0, 0))],
            out_specs=pl.BlockSpec((tt, d), lambda i, *_: (i, 0)),
            scratch_shapes=[pltpu.VMEM((2, _block_buffer_rows(tt), d), F32), pltpu.SemaphoreType.DMA((2,))]),
        out_shape=jax.ShapeDtypeStruct((n, d), F32),
        compiler_params=pltpu.CompilerParams(
            dimension_semantics=("arbitrary",), vmem_limit_bytes=VMEM_LIMIT),
        name="moe_combine",
    )(start, cnt, off, route, x2, ys, final_g.reshape(1, d))


def _moe(x2, h2, route, blk, totals, w1, b1, w2, b2, final_g, tt):
    n, d = x2.shape
    n_blk = n // tt
    tot = totals.astype(jnp.int32)
    padded = (tot + FFN_TILE - 1) // FFN_TILE * FFN_TILE
    ends = jnp.cumsum(padded)
    offs = ends - padded
    blk = blk[:, :, :N_EXPERTS].astype(jnp.int32)
    n_buf = _block_buffer_rows(tt)
    n_rows = -(-(n * TOP_K + n_blk * N_EXPERTS * ROW_ALIGN + N_EXPERTS * FFN_TILE + n_buf)
               // FFN_TILE) * FFN_TILE
    used = jnp.sum(blk[:, BLK_CNT], axis=1, keepdims=True)
    filler = n_buf - used
    col = lambda a, extra: jnp.concatenate([a, extra], axis=1).reshape(-1)
    cnt = col(blk[:, BLK_CNT], filler)
    off = col(blk[:, BLK_OFF], used)
    start = offs[None, :] + blk[:, BLK_BASE]
    start_out = col(start, jnp.full_like(used, n_rows - n_buf))
    start_in = col(start, jnp.zeros_like(used))
    n_tiles = n_rows // FFN_TILE
    tile_start = jnp.arange(n_tiles, dtype=jnp.int32) * FFN_TILE
    tile_expert = jnp.minimum(jnp.sum((tile_start[:, None] >= ends[None, :]).astype(jnp.int32), axis=1),
                              N_EXPERTS - 1)
    n_used = ends[-1:] // FFN_TILE

    xs = _dispatch(start_out, cnt, off, ends, padded, route, h2, n_rows, tt)
    ys = _ffn(tile_expert, n_used, xs, _w1_perm(w1), b1[:, None, 0::2], b1[:, None, 1::2],
              w2.astype(_MXU), b2[:, None, :])
    return _combine(start_in, cnt, off, route, x2, ys, final_g, tt)


def _tile(t_len, want):
    return want if t_len % want == 0 else t_len


def kernel(x, norm1_g, w_in, mu_shift, w0, w_up, a0, a_up, g_up, k_k, k_a, r_k, ln_x_w, ln_x_b,
           lambda_re, lambda_im, log_step, b_re, b_im, c_re, c_im, d_skip, w_glu, b_glu, beta_ssm,
           w_out, norm2_g, router_w, router_b, w1, b1, w2, b2, final_g):
    bsz, t_len, d = x.shape
    assert d == D_MODEL and t_len % CHUNK == 0 and norm1_g.shape[0] == 1
    tt = _tile(t_len, 256)
    a_re, a_im, bb_re, bb_im = _s5_params(lambda_re[0], lambda_im[0], log_step[0], b_re[0], b_im[0])
    r, lw, k, v, kk, b, g, u = _inproj(x, norm1_g[0], w_in[0], mu_shift[0], w0[0], w_up[0], a0[0],
                                       a_up[0], g_up[0], k_k[0], k_a[0], tt)
    y_rwkv = _rwkv(r, lw, k, v, kk, b, g, r_k[0], ln_x_w[0], ln_x_b[0], _tile(t_len, 512))
    y_ssm = _s5(jnp.swapaxes(u, 0, 1), a_re, a_im, bb_re, bb_im, c_re[0], c_im[0], d_skip[0],
                w_glu[0], b_glu[0], beta_ssm[0], _tile(t_len, 64))
    y_ssm = jnp.swapaxes(y_ssm, 0, 1)
    tb = _tile(t_len, 512)
    x2, h2, route, blk, totals = _outproj(x, y_rwkv, y_ssm, w_out[0], norm2_g[0], router_w[0],
                                          router_b[0], tb)
    n = bsz * t_len
    out = _moe(x2.reshape(n, d), h2.reshape(n, d), route.reshape(n, LANES), blk, totals[0, :N_EXPERTS],
               w1[0], b1[0], w2[0], b2[0], final_g, tb)
    return out.reshape(bsz, t_len, d)
```

```python
import functools

import jax
import jax.numpy as jnp
from jax import lax
from jax.experimental import pallas as pl
from jax.experimental.pallas import tpu as pltpu

F32 = jnp.float32
BF16 = jnp.bfloat16
_MXU = jnp.bfloat16

D_MODEL = 1024
D_RWKV = 512
D_SSM = 512
HEAD = 64
N_HEADS = D_RWKV // HEAD
LORA_W = 64
LORA_A = 64
LORA_G = 128
N_SHIFT = 3 * D_RWKV + LORA_W + LORA_A + LORA_G
D_IN = N_SHIFT + D_SSM
SSM_GROUP = 16
N_GROUPS = D_SSM // SSM_GROUP
SSM_STATE = 64
N_STATE = N_GROUPS * SSM_STATE
N_EXPERTS = 32
TOP_K = 4
D_FF = D_MODEL
SWIGLU_ALPHA = 1.702
SWIGLU_LIMIT = 7.0
NORM_EPS = 1e-5
LN_X_EPS = 64e-5

LANES = 128
CHUNK = 64
PAIR = 2 * HEAD
N_PAIRS = D_RWKV // PAIR
PREP_GROUP = 4
FFN_TILE = 512
VMEM_LIMIT = 56 * 1024 * 1024


def _mm(a, b):
    return jnp.dot(a.astype(_MXU), b.astype(_MXU), preferred_element_type=F32)


def _mm_nt(a, b):
    return lax.dot_general(a.astype(_MXU), b.astype(_MXU), (((1,), (1,)), ((), ())),
                           preferred_element_type=F32)


def _mm_tn(a, b):
    return lax.dot_general(a.astype(_MXU), b.astype(_MXU), (((0,), (0,)), ((), ())),
                           preferred_element_type=F32)


def _split3(a):
    h1 = a.astype(_MXU)
    r = a - h1.astype(F32)
    h2 = r.astype(_MXU)
    r = r - h2.astype(F32)
    return h1, h2, r.astype(_MXU)


def _mm_f32_lhs(a, m):
    mb = m.astype(_MXU)
    h1, h2, h3 = _split3(a)
    out = jnp.dot(h1, mb, preferred_element_type=F32)
    out = out + jnp.dot(h2, mb, preferred_element_type=F32)
    return out + jnp.dot(h3, mb, preferred_element_type=F32)


def _mm_f32_rhs(m, b):
    mb = m.astype(_MXU)
    h1, h2, h3 = _split3(b)
    out = jnp.dot(mb, h1, preferred_element_type=F32)
    out = out + jnp.dot(mb, h2, preferred_element_type=F32)
    return out + jnp.dot(mb, h3, preferred_element_type=F32)


def _mm_hi(a, b):
    a1 = a.astype(_MXU)
    a2 = (a - a1.astype(F32)).astype(_MXU)
    b1 = b.astype(_MXU)
    b2 = (b - b1.astype(F32)).astype(_MXU)
    out = jnp.dot(a1, b1, preferred_element_type=F32)
    out = out + jnp.dot(a1, b2, preferred_element_type=F32)
    return out + jnp.dot(a2, b1, preferred_element_type=F32)


def _sigmoid(z):
    return 1.0 / (1.0 + jnp.exp(-z))


def _rms(x, g):
    ms = jnp.mean(x * x, axis=-1, keepdims=True)
    return x * lax.rsqrt(ms + NORM_EPS) * g


MXU_TILE = 256


def _head_sum(x, scale):
    r = lax.broadcasted_iota(jnp.int32, (MXU_TILE, MXU_TILE), 0) // HEAD
    c = lax.broadcasted_iota(jnp.int32, (MXU_TILE, MXU_TILE), 1) // HEAD
    m = jnp.where(r == c, scale, 0.0).astype(_MXU)
    hi = x.astype(_MXU)
    lo = (x - hi.astype(F32)).astype(_MXU)
    parts = []
    for j in range(x.shape[1] // MXU_TILE):
        sl = slice(j * MXU_TILE, (j + 1) * MXU_TILE)
        parts.append(jnp.dot(hi[:, sl], m, preferred_element_type=F32)
                     + jnp.dot(lo[:, sl], m, preferred_element_type=F32))
    return jnp.concatenate(parts, axis=1)


def _s5_params_kernel(lre_ref, lim_ref, step_ref, bre_ref, bim_ref,
                      are_o, aim_o, bbre_o, bbim_o):
    lam_re = jnp.minimum(lre_ref[...], -1e-4)
    lam_im = lim_ref[...]
    dt = jnp.exp(step_ref[...])
    mag = jnp.exp(lam_re * dt)
    lb_re = mag * jnp.cos(lam_im * dt)
    lb_im = mag * jnp.sin(lam_im * dt)
    den = lam_re * lam_re + lam_im * lam_im
    num_re = lb_re - 1.0
    z_re = (num_re * lam_re + lb_im * lam_im) / den
    z_im = (lb_im * lam_re - num_re * lam_im) / den
    are_o[...] = lb_re
    aim_o[...] = lb_im
    pm = SSM_STATE * SSM_GROUP
    rep = (lax.broadcasted_iota(jnp.int32, (SSM_STATE, pm), 0)
           == lax.broadcasted_iota(jnp.int32, (SSM_STATE, pm), 1) // SSM_GROUP)
    rep = jnp.where(rep, 1.0, 0.0).astype(F32)
    zr = _mm_f32_lhs(z_re, rep)
    zi = _mm_f32_lhs(z_im, rep)
    b_re = bre_ref[...]
    b_im = bim_ref[...]
    bbre_o[...] = zr * b_re - zi * b_im
    bbim_o[...] = zr * b_im + zi * b_re


def _s5_params(lambda_re, lambda_im, log_step, b_re, b_im):
    g, p, m = b_re.shape
    outs = pl.pallas_call(
        _s5_params_kernel,
        out_shape=(jax.ShapeDtypeStruct((g, p), F32), jax.ShapeDtypeStruct((g, p), F32),
                   jax.ShapeDtypeStruct((g, p * m), F32), jax.ShapeDtypeStruct((g, p * m), F32)),
        name="s5_params",
    )(lambda_re, lambda_im, log_step.reshape(g, 1), b_re.reshape(g, p * m), b_im.reshape(g, p * m))
    a_re, a_im, bb_re, bb_im = outs
    return a_re, a_im, bb_re.reshape(g, p, m), bb_im.reshape(g, p, m)


def _inproj_kernel(x_ref, g1_ref, win_ref, mu_ref, w0_ref, wup_ref, a0_ref, aup_ref, gup_ref,
                   kk_ref, ka_ref,
                   r_o, lw_o, k_o, v_o, kk_o, b_o, g_o, u_o, carry_ref):
    ti = pl.program_id(1)
    h = _rms(x_ref[0], g1_ref[...])
    proj = _mm(h, win_ref[...])
    u_o[0] = proj[:, N_SHIFT:]
    p = proj[:, :N_SHIFT]
    tt = p.shape[0]
    carry = jnp.where(ti == 0, 0.0, carry_ref[0:1, :])
    row = lax.broadcasted_iota(jnp.int32, p.shape, 0)
    prev = jnp.where(row == 0, carry, pltpu.roll(p, 1, 0))
    carry_ref[0:1, :] = p[tt - 1:tt, :]
    ps = p + mu_ref[...] * (prev - p)

    r = ps[:, 0:D_RWKV]
    k = ps[:, D_RWKV:2 * D_RWKV]
    v = ps[:, 2 * D_RWKV:3 * D_RWKV]
    lora = ps[:, 3 * D_RWKV:3 * D_RWKV + LORA_W + LORA_A]
    g_lr = ps[:, 3 * D_RWKV + LORA_W + LORA_A:]

    z = w0_ref[...] + _mm(jnp.tanh(lora), wup_ref[...])
    nz = -z
    softplus = jnp.maximum(nz, 0.0) + jnp.log(1.0 + jnp.exp(-jnp.abs(nz)))
    log_w = -softplus - 0.5
    lw = -jnp.exp(log_w)
    a = _sigmoid(a0_ref[...] + _mm(lora, aup_ref[...]))
    g = _mm(_sigmoid(g_lr), gup_ref[...])

    kk = k * kk_ref[...]
    kk = kk / jnp.maximum(jnp.sqrt(_head_sum(kk * kk, 1.0)), 1e-12)
    k2 = k * (1.0 + (a - 1.0) * ka_ref[...])

    r_o[0] = r.astype(r_o.dtype)
    lw_o[0] = lw
    k_o[0] = k2.astype(k_o.dtype)
    v_o[0] = v.astype(v_o.dtype)
    kk_o[0] = kk.astype(kk_o.dtype)
    b_o[0] = (kk * a).astype(b_o.dtype)
    g_o[0] = g.astype(g_o.dtype)


def _inproj(x, norm1_g, w_in, mu_shift, w0, w_up, a0, a_up, g_up, k_k, k_a, tt):
    bsz, t_len, d = x.shape
    zeros_w = jnp.zeros((LORA_A, D_RWKV), F32)
    zeros_a = jnp.zeros((LORA_W, D_RWKV), F32)
    wup_pad = jnp.concatenate([w_up, zeros_w], axis=0).astype(_MXU)
    aup_pad = jnp.concatenate([zeros_a, a_up], axis=0).astype(_MXU)
    row = lambda v: v.reshape(1, -1)
    full = lambda shape: pl.BlockSpec(shape, lambda b, t: (0,) * len(shape))
    tok = lambda w: pl.BlockSpec((1, tt, w), lambda b, t: (b, t, 0))
    sds = lambda dt: jax.ShapeDtypeStruct((bsz, t_len, D_RWKV), dt)
    out_dtypes = [_MXU, F32, _MXU, _MXU, _MXU, _MXU, _MXU, F32]
    return pl.pallas_call(
        _inproj_kernel,
        grid=(bsz, t_len // tt),
        in_specs=[tok(d), full((1, d)), full((d, D_IN)), full((1, N_SHIFT)), full((1, D_RWKV)),
                  full((LORA_W + LORA_A, D_RWKV)), full((1, D_RWKV)), full((LORA_W + LORA_A, D_RWKV)),
                  full((LORA_G, D_RWKV)), full((1, D_RWKV)), full((1, D_RWKV))],
        out_specs=[tok(D_RWKV)] * 8,
        out_shape=[sds(dt) for dt in out_dtypes],
        scratch_shapes=[pltpu.VMEM((8, N_SHIFT), F32)],
        compiler_params=pltpu.CompilerParams(
            dimension_semantics=("arbitrary", "arbitrary"), vmem_limit_bytes=VMEM_LIMIT),
        name="inproj",
    )(x, row(norm1_g), w_in.astype(_MXU), row(mu_shift), row(w0), wup_pad, row(a0), aup_pad,
      g_up.astype(_MXU), row(k_k), row(k_a))


def _blockdiag(m, head0):
    return jnp.concatenate([jnp.where(head0, m, 0.0), jnp.where(head0, 0.0, m)], axis=0)


def _rwkv_kernel(r_ref, lw_ref, k_ref, v_ref, kk_ref, b_ref, g_ref, rk_ref, lnw_ref, lnb_ref,
                 y_o, s0_ref, s1_ref, s2_ref, s3_ref, y_scr,
                 w1_s, rt_s, arb_s, bh_s, kh_s, w2_s, ypar_s, gl_s):
    tb = pl.program_id(1)
    tc = r_ref.shape[1]
    n_chunks = tc // CHUNK

    @pl.when(tb == 0)
    def _():
        for s_ref in (s0_ref, s1_ref, s2_ref, s3_ref):
            s_ref[...] = jnp.zeros_like(s_ref)

    row = lax.broadcasted_iota(jnp.int32, (CHUNK, PAIR), 0)
    lane = lax.broadcasted_iota(jnp.int32, (CHUNK, PAIR), 1)
    head0 = lane < HEAD
    src = lane % HEAD
    strict = src < row
    incl = src <= row
    eye_pair = jnp.where(src == row, 1.0, 0.0).astype(F32)
    row2 = lax.broadcasted_iota(jnp.int32, (PAIR, PAIR), 0)
    lane2 = lax.broadcasted_iota(jnp.int32, (PAIR, PAIR), 1)
    same_head = (row2 < HEAD) == (lane2 < HEAD)
    tri = (lax.broadcasted_iota(jnp.int32, (CHUNK, CHUNK), 1)
           <= lax.broadcasted_iota(jnp.int32, (CHUNK, CHUNK), 0))
    tri = jnp.where(tri, 1.0, 0.0).astype(F32)

    pairs = range(N_PAIRS)
    psl = [slice(p * PAIR, (p + 1) * PAIR) for p in pairs]

    group = PREP_GROUP if n_chunks % PREP_GROUP == 0 else 1

    def prep_body(cg, carry):
        rows, a_t, r_t, b_t, k_t, v = [], [], [], [], [], []
        for j in range(group):
            c = cg * group + j
            rw = pl.ds(pl.multiple_of(c * CHUNK, CHUNK), CHUNK)
            lw = lw_ref[0, rw, :]
            cs = _mm_f32_rhs(tri, lw)
            cs_last = cs[CHUNK - 1:CHUNK, :]
            e_neg = jnp.exp(-cs)
            e_end = jnp.exp(cs_last - cs)
            kf = k_ref[0, rw, :].astype(F32)
            bf = b_ref[0, rw, :].astype(F32)
            rows.append(rw)
            r_t.append(r_ref[0, rw, :].astype(F32) * jnp.exp(cs))
            a_t.append(-kk_ref[0, rw, :].astype(F32) * jnp.exp(cs - lw))
            b_t.append(bf * e_neg)
            k_t.append(kf * e_neg)
            v.append(v_ref[0, rw, :].astype(F32))
            rt_s[rw, :] = r_t[j].astype(rt_s.dtype)
            bh_s[rw, :] = (bf * e_end).astype(bh_s.dtype)
            kh_s[rw, :] = (kf * e_end).astype(kh_s.dtype)
            gl_s[pl.ds(pl.multiple_of(c * 8, 8), 8), :] = jnp.broadcast_to(jnp.exp(cs_last), (8, D_RWKV))

        units = [(j, p) for j in range(group) for p in pairs]
        gram = [_mm_nt(jnp.concatenate([a_t[j][:, psl[p]], r_t[j][:, psl[p]]], axis=0),
                       jnp.concatenate([_blockdiag(b_t[j][:, psl[p]], head0),
                                        _blockdiag(k_t[j][:, psl[p]], head0)], axis=0)) for j, p in units]
        a_ab = [jnp.where(strict, gm[:CHUNK, :PAIR], 0.0) for gm in gram]
        a_ak = [jnp.where(strict, gm[:CHUNK, PAIR:], 0.0) for gm in gram]
        a_rb = [jnp.where(incl, gm[CHUNK:, :PAIR], 0.0) for gm in gram]
        a_rk = [jnp.where(incl, gm[CHUNK:, PAIR:], 0.0) for gm in gram]
        for i, (j, p) in enumerate(units):
            arb_s[rows[j], psl[p]] = a_rb[i].astype(arb_s.dtype)
        av = [_mm(jnp.concatenate([a_ak[i], a_rk[i]], axis=0), _blockdiag(v[j][:, psl[p]], head0))
              for i, (j, p) in enumerate(units)]
        for i, (j, p) in enumerate(units):
            ypar_s[rows[j], psl[p]] = av[i][CHUNK:]

        t_inv = [eye_pair + m for m in a_ab]
        m_pow = [_mm(m, _blockdiag(m, head0)) for m in a_ab]
        n_levels = CHUNK.bit_length() - 2
        for level in range(n_levels):
            if level + 1 < n_levels:
                out = [_mm(m, jnp.concatenate([_blockdiag(t, head0), _blockdiag(m, head0)], axis=1))
                       for t, m in zip(t_inv, m_pow)]
                t_inv = [t + o[:, :PAIR] for t, o in zip(t_inv, out)]
                m_pow = [o[:, PAIR:] for o in out]
            else:
                t_inv = [t + _mm(m, _blockdiag(t, head0)) for t, m in zip(t_inv, m_pow)]
        w12 = [_mm(t_inv[i], jnp.concatenate(
            [_blockdiag(a_t[j][:, psl[p]], head0), _blockdiag(av[i][:CHUNK], head0)], axis=1))
            for i, (j, p) in enumerate(units)]
        for i, (j, p) in enumerate(units):
            w1_s[rows[j], psl[p]] = w12[i][:, :PAIR].astype(w1_s.dtype)
            w2_s[rows[j], psl[p]] = w12[i][:, PAIR:]
        return carry

    lax.fori_loop(0, n_chunks // group, prep_body, 0)

    s_refs = (s0_ref, s1_ref, s2_ref, s3_ref)

    def state_body(c, carry):
        rows = pl.ds(pl.multiple_of(c * CHUNK, CHUNK), CHUNK)
        g_rows = pl.ds(pl.multiple_of(c * 8, 8), 8)
        s0 = [s_refs[p][...] for p in pairs]
        uy = [_mm_nt(jnp.concatenate([w1_s[rows, psl[p]], rt_s[rows, psl[p]]], axis=0), s0[p])
              for p in pairs]
        u = [uy[p][:CHUNK] + w2_s[rows, psl[p]] for p in pairs]
        y = [uy[p][CHUNK:] + ypar_s[rows, psl[p]] + _mm(arb_s[rows, psl[p]], _blockdiag(u[p], head0))
             for p in pairs]
        upd = [_mm_tn(jnp.concatenate([u[p], v_ref[0, rows, psl[p]]], axis=0),
                      jnp.concatenate([bh_s[rows, psl[p]], kh_s[rows, psl[p]]], axis=0)) for p in pairs]
        for p in pairs:
            y_scr[rows, psl[p]] = y[p]
            decay = jnp.concatenate([gl_s[g_rows, psl[p]]] * (PAIR // 8), axis=0)
            s_refs[p][...] = s0[p] * decay + jnp.where(same_head, upd[p], 0.0)
        return carry

    lax.fori_loop(0, n_chunks, state_body, 0)

    y = y_scr[...]
    mu = _head_sum(y, 1.0 / HEAD)
    d = y - mu
    var = _head_sum(d * d, 1.0 / HEAD)
    yn = d * lax.rsqrt(var + LN_X_EPS) * lnw_ref[...] + lnb_ref[...]
    rk = r_ref[0].astype(F32) * k_ref[0].astype(F32) * rk_ref[...]
    bonus = _head_sum(rk, 1.0) * v_ref[0].astype(F32)
    y_o[0] = ((yn + bonus) * g_ref[0].astype(F32)).astype(y_o.dtype)


def _rwkv(r, lw, k, v, kk, b, g, r_k, ln_w, ln_b, tc):
    bsz, t_len, _ = r.shape
    tok = pl.BlockSpec((1, tc, D_RWKV), lambda bb, t: (bb, t, 0))
    par = pl.BlockSpec((1, D_RWKV), lambda bb, t: (0, 0))
    return pl.pallas_call(
        _rwkv_kernel,
        grid=(bsz, t_len // tc),
        in_specs=[tok] * 7 + [par] * 3,
        out_specs=tok,
        out_shape=jax.ShapeDtypeStruct((bsz, t_len, D_RWKV), _MXU),
        scratch_shapes=([pltpu.VMEM((PAIR, PAIR), F32)] * N_PAIRS + [pltpu.VMEM((tc, D_RWKV), F32)]
                        + [pltpu.VMEM((tc, D_RWKV), _MXU)] * 5 + [pltpu.VMEM((tc, D_RWKV), F32)] * 2
                        + [pltpu.VMEM((tc // CHUNK * 8, D_RWKV), F32)]),
        compiler_params=pltpu.CompilerParams(
            dimension_semantics=("arbitrary", "arbitrary"), vmem_limit_bytes=VMEM_LIMIT),
        name="rwkv",
    )(r, lw, k, v, kk, b, g, r_k.reshape(1, D_RWKV), ln_w.reshape(1, D_RWKV), ln_b.reshape(1, D_RWKV))


def _s5_kernel(u_ref, unext_ref, are_ref, aim_ref, bbd_ref, cbd_ref, dskip_ref, wglu_ref, bglu_ref,
               beta_ref, y_o, sre_ref, sim_ref, bu0_ref, bu1_ref):
    i = pl.program_id(0)
    lc, bsz, _ = u_ref.shape

    def input_map(src_ref, dst_ref):
        ub = src_ref[...].reshape(lc * bsz, D_SSM).astype(_MXU)
        chan_per_tile = MXU_TILE // SSM_STATE * SSM_GROUP
        for j in range(2 * N_STATE // MXU_TILE):
            c0 = (j * MXU_TILE % N_STATE) // MXU_TILE * chan_per_tile // MXU_TILE * MXU_TILE
            cols = slice(j * MXU_TILE, (j + 1) * MXU_TILE)
            dst_ref[:, cols] = jnp.dot(ub[:, c0:c0 + MXU_TILE], bbd_ref[c0:c0 + MXU_TILE, cols],
                                       preferred_element_type=F32)

    @pl.when(i == 0)
    def _():
        sre_ref[...] = jnp.zeros_like(sre_ref)
        sim_ref[...] = jnp.zeros_like(sim_ref)
        input_map(u_ref, bu0_ref)

    def block_body(bu_ref, next_ref):
        input_map(unext_ref, next_ref)
        a_re = jnp.broadcast_to(are_ref[...], (bsz, N_STATE))
        a_im = jnp.broadcast_to(aim_ref[...], (bsz, N_STATE))

        def step(t, carry):
            s_re, s_im = carry
            rows = pl.ds(t * bsz, bsz)
            n_re = a_re * s_re - a_im * s_im + bu_ref[rows, :N_STATE]
            n_im = a_re * s_im + a_im * s_re + bu_ref[rows, N_STATE:]
            bu_ref[rows, :N_STATE] = n_re
            bu_ref[rows, N_STATE:] = n_im
            return n_re, n_im

        s_re, s_im = lax.fori_loop(0, lc, step, (sre_ref[...], sim_ref[...]), unroll=True)
        sre_ref[...] = s_re
        sim_ref[...] = s_im

        state_per_tile = MXU_TILE // SSM_GROUP * SSM_STATE
        parts = []
        for n in range(D_SSM // MXU_TILE):
            cols = slice(n * MXU_TILE, (n + 1) * MXU_TILE)
            re_rows = slice(n * state_per_tile, (n + 1) * state_per_tile)
            im_rows = slice(N_STATE + n * state_per_tile, N_STATE + (n + 1) * state_per_tile)
            parts.append(_mm(bu_ref[:, re_rows], cbd_ref[re_rows, cols])
                         + _mm(bu_ref[:, im_rows], cbd_ref[im_rows, cols]))
        y = jnp.concatenate(parts, axis=1) + dskip_ref[...] * u_ref[...].reshape(lc * bsz, D_SSM)
        y = jax.nn.gelu(y)
        y = y * _sigmoid(_mm(y, wglu_ref[...]) + bglu_ref[...])
        y_o[...] = _rms(y, beta_ref[...]).reshape(lc, bsz, D_SSM)

    @pl.when(i % 2 == 0)
    def _():
        block_body(bu0_ref, bu1_ref)

    @pl.when(i % 2 == 1)
    def _():
        block_body(bu1_ref, bu0_ref)


def _s5(u_tm, a_re, a_im, bb_re, bb_im, c_re, c_im, d_skip, w_glu, b_glu, beta, lc):
    t_len, bsz, _ = u_tm.shape
    eye = jnp.eye(N_GROUPS, dtype=F32)
    bbd = jnp.concatenate(
        [jnp.einsum('gpm,gh->gmhp', bb_re, eye).reshape(D_SSM, N_STATE),
         jnp.einsum('gpm,gh->gmhp', bb_im, eye).reshape(D_SSM, N_STATE)], axis=1).astype(_MXU)
    cbd = jnp.concatenate(
        [jnp.einsum('gmp,gh->gphm', c_re, eye).reshape(N_STATE, D_SSM),
         jnp.einsum('gmp,gh->gphm', -c_im, eye).reshape(N_STATE, D_SSM)], axis=0).astype(_MXU)
    full = lambda shape: pl.BlockSpec(shape, lambda i: (0,) * len(shape))
    n_steps = t_len // lc
    tok = pl.BlockSpec((lc, bsz, D_SSM), lambda i: (i, 0, 0))
    tok_next = pl.BlockSpec((lc, bsz, D_SSM), lambda i: (jnp.minimum(i + 1, n_steps - 1), 0, 0))
    return pl.pallas_call(
        _s5_kernel,
        grid=(n_steps,),
        in_specs=[tok, tok_next, full((1, N_STATE)), full((1, N_STATE)), full((D_SSM, 2 * N_STATE)),
                  full((2 * N_STATE, D_SSM)), full((1, D_SSM)), full((D_SSM, D_SSM)),
                  full((1, D_SSM)), full((1, D_SSM))],
        out_specs=tok,
        out_shape=jax.ShapeDtypeStruct((t_len, bsz, D_SSM), F32),
        scratch_shapes=[pltpu.VMEM((bsz, N_STATE), F32), pltpu.VMEM((bsz, N_STATE), F32),
                        pltpu.VMEM((lc * bsz, 2 * N_STATE), F32), pltpu.VMEM((lc * bsz, 2 * N_STATE), F32)],
        compiler_params=pltpu.CompilerParams(
            dimension_semantics=("arbitrary",), vmem_limit_bytes=VMEM_LIMIT),
        name="s5",
    )(u_tm, u_tm, a_re.reshape(1, N_STATE), a_im.reshape(1, N_STATE), bbd, cbd, d_skip.reshape(1, D_SSM),
      w_glu.astype(_MXU), b_glu.reshape(1, D_SSM), beta.reshape(1, D_SSM))


ROUTE_IDX = 0
ROUTE_COL = TOP_K
ROUTE_GATE = 2 * TOP_K
ROW_ALIGN = 8
BLK_BASE, BLK_CNT, BLK_OFF = 0, 1, 2


def _outproj_kernel(x_ref, yr_ref, ys_ref, wout_ref, g2_ref, rw_ref, rb_ref,
                    x2_o, h2_o, route_o, blk_o, tot_o, carry_ref):
    first = jnp.logical_and(pl.program_id(0) == 0, pl.program_id(1) == 0)

    @pl.when(first)
    def _():
        carry_ref[...] = jnp.zeros_like(carry_ref)

    x2 = (x_ref[0] + _mm(yr_ref[0], wout_ref[:D_RWKV, :]) + _mm(ys_ref[0], wout_ref[D_RWKV:, :]))
    x2_o[0] = x2
    h2 = _rms(x2, g2_ref[...])
    h2_o[0] = h2.astype(h2_o.dtype)
    logits = _mm_hi(h2, rw_ref[...]) + rb_ref[...]
    tt = logits.shape[0]
    lane = lax.broadcasted_iota(jnp.int32, (tt, LANES), 1).astype(F32)

    vals, idxs = [], []
    member = jnp.zeros((tt, LANES), F32)
    for _ in range(TOP_K):
        m = jnp.max(logits, axis=-1, keepdims=True)
        idx = jnp.min(jnp.where(logits == m, lane, float(LANES)), axis=-1, keepdims=True)
        sel = lane == idx
        member = jnp.where(sel, 1.0, member)
        logits = jnp.where(sel, -jnp.inf, logits)
        vals.append(m)
        idxs.append(idx)
    exps = [jnp.exp(v - vals[0]) for v in vals]
    denom = exps[0] + exps[1] + exps[2] + exps[3]

    before = (lax.broadcasted_iota(jnp.int32, (tt, tt), 1)
              < lax.broadcasted_iota(jnp.int32, (tt, tt), 0))
    rank_local = _mm(jnp.where(before, 1.0, 0.0), member)
    cnt = jnp.sum(member, axis=0, keepdims=True)
    cnt_al = jnp.floor((cnt + (ROW_ALIGN - 1.0)) * (1.0 / ROW_ALIGN)) * ROW_ALIGN
    upper = (lax.broadcasted_iota(jnp.int32, (LANES, LANES), 0)
             < lax.broadcasted_iota(jnp.int32, (LANES, LANES), 1))
    run_off = _mm_f32_lhs(jnp.broadcast_to(cnt_al, (8, LANES)), jnp.where(upper, 1.0, 0.0))[0:1, :]
    col_all = rank_local + run_off
    base = carry_ref[...]
    carry_ref[...] = base + cnt_al
    tot_o[...] = jnp.broadcast_to(base + cnt_al, tot_o.shape)
    rows8 = lax.broadcasted_iota(jnp.int32, (8, LANES), 0)
    blk_o[0] = jnp.where(rows8 == BLK_BASE, base,
                         jnp.where(rows8 == BLK_CNT, cnt_al, jnp.where(rows8 == BLK_OFF, run_off, 0.0)))

    route = jnp.zeros((tt, LANES), F32)
    for k in range(TOP_K):
        col_k = jnp.sum(jnp.where(lane == idxs[k], col_all, 0.0), axis=-1, keepdims=True)
        route = jnp.where(lane == ROUTE_IDX + k, idxs[k], route)
        route = jnp.where(lane == ROUTE_COL + k, col_k, route)
        route = jnp.where(lane == ROUTE_GATE + k, exps[k] / denom, route)
    route_o[0] = route


def _outproj(x, y_rwkv, y_ssm, w_out, norm2_g, router_w, router_b, tt):
    bsz, t_len, d = x.shape
    nt = t_len // tt
    rw = jnp.zeros((d, LANES), F32).at[:, :N_EXPERTS].set(router_w)
    rb = jnp.full((1, LANES), -1e30, F32).at[0, :N_EXPERTS].set(router_b)
    full = lambda shape: pl.BlockSpec(shape, lambda b, t: (0,) * len(shape))
    tok = lambda w: pl.BlockSpec((1, tt, w), lambda b, t: (b, t, 0))
    return pl.pallas_call(
        _outproj_kernel,
        grid=(bsz, nt),
        in_specs=[tok(d), tok(D_RWKV), tok(D_SSM), full((d, d)), full((1, d)), full((d, LANES)),
                  full((1, LANES))],
        out_specs=[tok(d), tok(d), tok(LANES), pl.BlockSpec((1, 8, LANES), lambda b, t: (b * nt + t, 0, 0)),
                   full((8, LANES))],
        out_shape=[jax.ShapeDtypeStruct((bsz, t_len, d), F32), jax.ShapeDtypeStruct((bsz, t_len, d), _MXU),
                   jax.ShapeDtypeStruct((bsz, t_len, LANES), F32),
                   jax.ShapeDtypeStruct((bsz * nt, 8, LANES), F32), jax.ShapeDtypeStruct((8, LANES), F32)],
        scratch_shapes=[pltpu.VMEM((1, LANES), F32)],
        compiler_params=pltpu.CompilerParams(
            dimension_semantics=("arbitrary", "arbitrary"), vmem_limit_bytes=VMEM_LIMIT),
        name="outproj_router",
    )(x, y_rwkv, y_ssm, w_out.astype(_MXU), norm2_g.reshape(1, d), rw, rb)


def _zero_tile_copy(z_ref, xs_ref, sem, start):
    return pltpu.make_async_copy(z_ref, xs_ref.at[pl.ds(start, FFN_TILE)], sem)


def _run_sizes(tt):
    sizes, s = [], ROW_ALIGN
    while s <= tt:
        sizes.append(s)
        s *= 2
    return sizes[::-1]


N_RUNS = N_EXPERTS + 1


def _for_each_run_piece(blk, tt, start_ref, cnt_ref, off_ref, fn):
    def per_expert(e, carry):
        j = blk * N_RUNS + e
        cnt, off, start = cnt_ref[j], off_ref[j], start_ref[j]
        done = 0
        for size in _run_sizes(tt):
            @pl.when((cnt & size) != 0)
            def _(done=done, size=size):
                fn(pl.multiple_of(off + done, ROW_ALIGN), pl.multiple_of(start + done, ROW_ALIGN), size)
            done = done + (cnt & size)
        return carry

    lax.fori_loop(0, N_RUNS, per_expert, 0)


def _dispatch_kernel(start_ref, cnt_ref, off_ref, ends_ref, padded_ref, route_ref, h_ref, xs_o,
                     x_buf, z_ref, sems, z_sem):
    i = pl.program_id(0)
    n_blk = pl.num_programs(0)
    tt = h_ref.shape[0]
    n_buf = x_buf.shape[1]
    slot = i % 2

    def start_pieces(buf_slot):
        def fn(buf_row, sorted_row, size):
            pltpu.make_async_copy(x_buf.at[buf_slot, pl.ds(buf_row, size)],
                                  xs_o.at[pl.ds(sorted_row, size)], sems.at[buf_slot]).start()
        return fn

    def wait_block(buf_slot):
        pltpu.make_async_copy(x_buf.at[buf_slot], xs_o.at[pl.ds(0, n_buf)], sems.at[buf_slot]).wait()

    @pl.when(i == 0)
    def _():
        z_ref[...] = jnp.zeros_like(z_ref)
        for e in range(N_EXPERTS):
            @pl.when(padded_ref[e] > 0)
            def _():
                start = pl.multiple_of(ends_ref[e] - FFN_TILE, FFN_TILE)
                _zero_tile_copy(z_ref, xs_o, z_sem, start).start()
        n_used = ends_ref[N_EXPERTS - 1] // FFN_TILE
        n_tiles = xs_o.shape[0] // FFN_TILE

        def zero_tail(j, carry):
            _zero_tile_copy(z_ref, xs_o, z_sem, pl.multiple_of(j * FFN_TILE, FFN_TILE)).start()
            return carry

        def wait_tail(j, carry):
            _zero_tile_copy(z_ref, xs_o, z_sem, 0).wait()
            return carry

        lax.fori_loop(n_used, n_tiles, zero_tail, 0)
        for e in range(N_EXPERTS):
            @pl.when(padded_ref[e] > 0)
            def _():
                _zero_tile_copy(z_ref, xs_o, z_sem, 0).wait()
        lax.fori_loop(n_used, n_tiles, wait_tail, 0)

    route_t = route_ref[...].T
    rows = lax.broadcasted_iota(jnp.int32, (n_buf, tt), 0).astype(F32)
    onehot = jnp.zeros((n_buf, tt), F32)
    for k in range(TOP_K):
        onehot = jnp.where(rows == route_t[ROUTE_COL + k:ROUTE_COL + k + 1, :], 1.0, onehot)
    x_buf[slot] = _mm(onehot, h_ref[...])

    @pl.when(i > 0)
    def _():
        wait_block(1 - slot)

    _for_each_run_piece(i, tt, start_ref, cnt_ref, off_ref, start_pieces(slot))

    @pl.when(i == n_blk - 1)
    def _():
        wait_block(slot)


def _block_buffer_rows(tt):
    return TOP_K * tt + N_EXPERTS * ROW_ALIGN


def _dispatch(start, cnt, off, ends, padded, route, h2, n_rows, tt):
    n, d = h2.shape
    return pl.pallas_call(
        _dispatch_kernel,
        grid_spec=pltpu.PrefetchScalarGridSpec(
            num_scalar_prefetch=5,
            grid=(n // tt,),
            in_specs=[pl.BlockSpec((tt, LANES), lambda i, *_: (i, 0)),
                      pl.BlockSpec((tt, d), lambda i, *_: (i, 0))],
            out_specs=pl.BlockSpec(memory_space=pl.ANY),
            scratch_shapes=[pltpu.VMEM((2, _block_buffer_rows(tt), d), F32), pltpu.VMEM((FFN_TILE, d), F32),
                            pltpu.SemaphoreType.DMA((2,)), pltpu.SemaphoreType.DMA]),
        out_shape=jax.ShapeDtypeStruct((n_rows, d), F32),
        compiler_params=pltpu.CompilerParams(
            dimension_semantics=("arbitrary",), vmem_limit_bytes=VMEM_LIMIT),
        name="moe_dispatch",
    )(start, cnt, off, ends, padded, route, h2)


def _w1_perm_kernel(w_ref, o_ref):
    r = lax.broadcasted_iota(jnp.int32, (MXU_TILE, MXU_TILE), 0)
    c = lax.broadcasted_iota(jnp.int32, (MXU_TILE, MXU_TILE), 1)
    half = MXU_TILE // 2
    src = jnp.where(c < half, 2 * c, 2 * (c - half) + 1)
    perm = jnp.where(r == src, 1.0, 0.0).astype(_MXU)
    for j in range(w_ref.shape[2] // MXU_TILE):
        sl = slice(j * MXU_TILE, (j + 1) * MXU_TILE)
        o_ref[0, :, sl] = jnp.dot(w_ref[0, :, sl].astype(_MXU), perm,
                                  preferred_element_type=F32).astype(o_ref.dtype)


def _w1_perm(w1):
    e, d, f2 = w1.shape
    spec = pl.BlockSpec((1, d, f2), lambda i: (i, 0, 0))
    return pl.pallas_call(
        _w1_perm_kernel,
        grid=(e,),
        in_specs=[spec],
        out_specs=spec,
        out_shape=jax.ShapeDtypeStruct(w1.shape, _MXU),
        compiler_params=pltpu.CompilerParams(
            dimension_semantics=("arbitrary",), vmem_limit_bytes=VMEM_LIMIT),
        name="w1_regroup",
    )(w1)


def _ffn_kernel(te_ref, nused_ref, xs_ref, w1_ref, b1g_ref, b1l_ref, w2_ref, b2_ref, ys_o):
    i = pl.program_id(0)

    @pl.when(i < nused_ref[0])
    def _():
        hu = _mm(xs_ref[...], w1_ref[0])
        half = MXU_TILE // 2
        n_grp = hu.shape[1] // MXU_TILE
        h_glu = jnp.concatenate([hu[:, j * MXU_TILE:j * MXU_TILE + half] for j in range(n_grp)], axis=1)
        h_lin = jnp.concatenate([hu[:, j * MXU_TILE + half:(j + 1) * MXU_TILE] for j in range(n_grp)], axis=1)
        x_glu = jnp.minimum(h_glu + b1g_ref[0], SWIGLU_LIMIT)
        x_lin = jnp.clip(h_lin + b1l_ref[0], -SWIGLU_LIMIT, SWIGLU_LIMIT)
        act = x_glu * _sigmoid(SWIGLU_ALPHA * x_glu) * (x_lin + 1.0)
        ys_o[...] = _mm(act, w2_ref[0]) + b2_ref[0]

    @pl.when(i >= nused_ref[0])
    def _():
        ys_o[...] = jnp.zeros_like(ys_o)


def _ffn(tile_expert, n_used, xs, w1p, b1g, b1l, w2, b2):
    n_rows, d = xs.shape
    n_tiles = n_rows // FFN_TILE

    def row_map(i, te, nu):
        return (jnp.minimum(i, nu[0] - 1), 0)

    def exp_map(i, te, nu):
        return (te[i], 0, 0)

    return pl.pallas_call(
        _ffn_kernel,
        grid_spec=pltpu.PrefetchScalarGridSpec(
            num_scalar_prefetch=2,
            grid=(n_tiles,),
            in_specs=[pl.BlockSpec((FFN_TILE, d), row_map),
                      pl.BlockSpec((1, d, 2 * D_FF), exp_map),
                      pl.BlockSpec((1, 1, D_FF), exp_map), pl.BlockSpec((1, 1, D_FF), exp_map),
                      pl.BlockSpec((1, D_FF, d), exp_map), pl.BlockSpec((1, 1, d), exp_map)],
            out_specs=pl.BlockSpec((FFN_TILE, d), lambda i, te, nu: (i, 0))),
        out_shape=jax.ShapeDtypeStruct((n_rows, d), F32),
        compiler_params=pltpu.CompilerParams(
            dimension_semantics=("arbitrary",), vmem_limit_bytes=VMEM_LIMIT),
        name="moe_ffn",
    )(tile_expert, n_used, xs, w1p, b1g, b1l, w2, b2)


def _combine_kernel(start_ref, cnt_ref, off_ref, route_ref, x2_ref, ys_hbm, fg_ref, out_o, y_buf, sems):
    i = pl.program_id(0)
    n_blk = pl.num_programs(0)
    tt = x2_ref.shape[0]
    n_buf = y_buf.shape[1]
    slot = i % 2

    def start_pieces(buf_slot):
        def fn(buf_row, sorted_row, size):
            pltpu.make_async_copy(ys_hbm.at[pl.ds(sorted_row, size)],
                                  y_buf.at[buf_slot, pl.ds(buf_row, size)], sems.at[buf_slot]).start()
        return fn

    @pl.when(i == 0)
    def _():
        _for_each_run_piece(0, tt, start_ref, cnt_ref, off_ref, start_pieces(0))

    @pl.when(i + 1 < n_blk)
    def _():
        _for_each_run_piece(i + 1, tt, start_ref, cnt_ref, off_ref, start_pieces(1 - slot))

    pltpu.make_async_copy(ys_hbm.at[pl.ds(0, n_buf)], y_buf.at[slot], sems.at[slot]).wait()

    route = route_ref[...]
    cols = lax.broadcasted_iota(jnp.int32, (tt, n_buf), 1).astype(F32)
    gates = jnp.zeros((tt, n_buf), F32)
    for k in range(TOP_K):
        gates = jnp.where(cols == route[:, ROUTE_COL + k:ROUTE_COL + k + 1],
                          route[:, ROUTE_GATE + k:ROUTE_GATE + k + 1], gates)
    acc = x2_ref[...] + _mm(gates, y_buf[slot])
    out_o[...] = _rms(acc, fg_ref[...])


def _combine(start, cnt, off, route, x2, ys, final_g, tt):
    n, d = x2.shape
    return pl.pallas_call(
        _combine_kernel,
        grid_spec=pltpu.PrefetchScalarGridSpec(
            num_scalar_prefetch=3,
            grid=(n // tt,),
            in_specs=[pl.BlockSpec((tt, LANES), lambda i, *_: (i, 0)),
                      pl.BlockSpec((tt, d), lambda i, *_: (i, 0)),
                      pl.BlockSpec(memory_space=pl.ANY), pl.BlockSpec((1, d), lambda i, *_: (0, 0))],
            out_specs=pl.BlockSpec((tt, d), lambda i, *_: (i, 0)),
            scratch_shapes=[pltpu.VMEM((2, _block_buffer_rows(tt), d), F32), pltpu.SemaphoreType.DMA((2,))]),
        out_shape=jax.ShapeDtypeStruct((n, d), F32),
        compiler_params=pltpu.CompilerParams(
            dimension_semantics=("arbitrary",), vmem_limit_bytes=VMEM_LIMIT),
        name="moe_combine",
    )(start, cnt, off, route, x2, ys, final_g.reshape(1, d))


def _moe(x2, h2, route, blk, totals, w1, b1, w2, b2, final_g, tt):
    n, d = x2.shape
    n_blk = n // tt
    tot = totals.astype(jnp.int32)
    padded = (tot + FFN_TILE - 1) // FFN_TILE * FFN_TILE
    ends = jnp.cumsum(padded)
    offs = ends - padded
    blk = blk[:, :, :N_EXPERTS].astype(jnp.int32)
    n_buf = _block_buffer_rows(tt)
    n_rows = -(-(n * TOP_K + n_blk * N_EXPERTS * ROW_ALIGN + N_EXPERTS * FFN_TILE + n_buf)
               // FFN_TILE) * FFN_TILE
    used = jnp.sum(blk[:, BLK_CNT], axis=1, keepdims=True)
    filler = n_buf - used
    col = lambda a, extra: jnp.concatenate([a, extra], axis=1).reshape(-1)
    cnt = col(blk[:, BLK_CNT], filler)
    off = col(blk[:, BLK_OFF], used)
    start = offs[None, :] + blk[:, BLK_BASE]
    start_out = col(start, jnp.full_like(used, n_rows - n_buf))
    start_in = col(start, jnp.zeros_like(used))
    n_tiles = n_rows // FFN_TILE
    tile_start = jnp.arange(n_tiles, dtype=jnp.int32) * FFN_TILE
    tile_expert = jnp.minimum(jnp.sum((tile_start[:, None] >= ends[None, :]).astype(jnp.int32), axis=1),
                              N_EXPERTS - 1)
    n_used = ends[-1:] // FFN_TILE

    xs = _dispatch(start_out, cnt, off, ends, padded, route, h2, n_rows, tt)
    ys = _ffn(tile_expert, n_used, xs, _w1_perm(w1), b1[:, None, 0::2], b1[:, None, 1::2],
              w2.astype(_MXU), b2[:, None, :])
    return _combine(start_in, cnt, off, route, x2, ys, final_g, tt)


def _tile(t_len, want):
    return want if t_len % want == 0 else t_len


def kernel(x, norm1_g, w_in, mu_shift, w0, w_up, a0, a_up, g_up, k_k, k_a, r_k, ln_x_w, ln_x_b,
           lambda_re, lambda_im, log_step, b_re, b_im, c_re, c_im, d_skip, w_glu, b_glu, beta_ssm,
           w_out, norm2_g, router_w, router_b, w1, b1, w2, b2, final_g):
    bsz, t_len, d = x.shape
    assert d == D_MODEL and t_len % CHUNK == 0 and norm1_g.shape[0] == 1
    tt = _tile(t_len, 256)
    a_re, a_im, bb_re, bb_im = _s5_params(lambda_re[0], lambda_im[0], log_step[0], b_re[0], b_im[0])
    r, lw, k, v, kk, b, g, u = _inproj(x, norm1_g[0], w_in[0], mu_shift[0], w0[0], w_up[0], a0[0],
                                       a_up[0], g_up[0], k_k[0], k_a[0], tt)
    y_rwkv = _rwkv(r, lw, k, v, kk, b, g, r_k[0], ln_x_w[0], ln_x_b[0], _tile(t_len, 512))
    y_ssm = _s5(jnp.swapaxes(u, 0, 1), a_re, a_im, bb_re, bb_im, c_re[0], c_im[0], d_skip[0],
                w_glu[0], b_glu[0], beta_ssm[0], _tile(t_len, 64))
    y_ssm = jnp.swapaxes(y_ssm, 0, 1)
    tb = _tile(t_len, 512)
    x2, h2, route, blk, totals = _outproj(x, y_rwkv, y_ssm, w_out[0], norm2_g[0], router_w[0],
                                          router_b[0], tb)
    n = bsz * t_len
    out = _moe(x2.reshape(n, d), h2.reshape(n, d), route.reshape(n, LANES), blk, totals[0, :N_EXPERTS],
               w1[0], b1[0], w2[0], b2[0], final_g, tb)
    return out.reshape(bsz, t_len, d)
```

```python
import functools

import jax
import jax.numpy as jnp
from jax import lax
from jax.experimental import pallas as pl
from jax.experimental.pallas import tpu as pltpu

F32 = jnp.float32
BF16 = jnp.bfloat16
_MXU = jnp.bfloat16

D_MODEL = 1024
D_RWKV = 512
D_SSM = 512
HEAD = 64
N_HEADS = D_RWKV // HEAD
LORA_W = 64
LORA_A = 64
LORA_G = 128
N_SHIFT = 3 * D_RWKV + LORA_W + LORA_A + LORA_G
D_IN = N_SHIFT + D_SSM
SSM_GROUP = 16
N_GROUPS = D_SSM // SSM_GROUP
SSM_STATE = 64
N_STATE = N_GROUPS * SSM_STATE
N_EXPERTS = 32
TOP_K = 4
D_FF = D_MODEL
SWIGLU_ALPHA = 1.702
SWIGLU_LIMIT = 7.0
NORM_EPS = 1e-5
LN_X_EPS = 64e-5

LANES = 128
CHUNK = 64
PAIR = 2 * HEAD
N_PAIRS = D_RWKV // PAIR
PREP_GROUP = 4
FFN_TILE = 512
VMEM_LIMIT = 56 * 1024 * 1024


def _mm(a, b):
    return jnp.dot(a.astype(_MXU), b.astype(_MXU), preferred_element_type=F32)


def _mm_nt(a, b):
    return lax.dot_general(a.astype(_MXU), b.astype(_MXU), (((1,), (1,)), ((), ())),
                           preferred_element_type=F32)


def _mm_tn(a, b):
    return lax.dot_general(a.astype(_MXU), b.astype(_MXU), (((0,), (0,)), ((), ())),
                           preferred_element_type=F32)


def _split3(a):
    h1 = a.astype(_MXU)
    r = a - h1.astype(F32)
    h2 = r.astype(_MXU)
    r = r - h2.astype(F32)
    return h1, h2, r.astype(_MXU)


def _mm_f32_lhs(a, m):
    mb = m.astype(_MXU)
    h1, h2, h3 = _split3(a)
    out = jnp.dot(h1, mb, preferred_element_type=F32)
    out = out + jnp.dot(h2, mb, preferred_element_type=F32)
    return out + jnp.dot(h3, mb, preferred_element_type=F32)


def _mm_f32_rhs(m, b):
    mb = m.astype(_MXU)
    h1, h2, h3 = _split3(b)
    out = jnp.dot(mb, h1, preferred_element_type=F32)
    out = out + jnp.dot(mb, h2, preferred_element_type=F32)
    return out + jnp.dot(mb, h3, preferred_element_type=F32)


def _mm_hi(a, b):
    a1 = a.astype(_MXU)
    a2 = (a - a1.astype(F32)).astype(_MXU)
    b1 = b.astype(_MXU)
    b2 = (b - b1.astype(F32)).astype(_MXU)
    out = jnp.dot(a1, b1, preferred_element_type=F32)
    out = out + jnp.dot(a1, b2, preferred_element_type=F32)
    return out + jnp.dot(a2, b1, preferred_element_type=F32)


def _sigmoid(z):
    return 1.0 / (1.0 + jnp.exp(-z))


def _rms(x, g):
    ms = jnp.mean(x * x, axis=-1, keepdims=True)
    return x * lax.rsqrt(ms + NORM_EPS) * g


MXU_TILE = 256


def _head_sum(x, scale):
    r = lax.broadcasted_iota(jnp.int32, (MXU_TILE, MXU_TILE), 0) // HEAD
    c = lax.broadcasted_iota(jnp.int32, (MXU_TILE, MXU_TILE), 1) // HEAD
    m = jnp.where(r == c, scale, 0.0).astype(_MXU)
    hi = x.astype(_MXU)
    lo = (x - hi.astype(F32)).astype(_MXU)
    parts = []
    for j in range(x.shape[1] // MXU_TILE):
        sl = slice(j * MXU_TILE, (j + 1) * MXU_TILE)
        parts.append(jnp.dot(hi[:, sl], m, preferred_element_type=F32)
                     + jnp.dot(lo[:, sl], m, preferred_element_type=F32))
    return jnp.concatenate(parts, axis=1)


def _s5_params_kernel(lre_ref, lim_ref, step_ref, bre_ref, bim_ref,
                      are_o, aim_o, bbre_o, bbim_o):
    lam_re = jnp.minimum(lre_ref[...], -1e-4)
    lam_im = lim_ref[...]
    dt = jnp.exp(step_ref[...])
    mag = jnp.exp(lam_re * dt)
    lb_re = mag * jnp.cos(lam_im * dt)
    lb_im = mag * jnp.sin(lam_im * dt)
    den = lam_re * lam_re + lam_im * lam_im
    num_re = lb_re - 1.0
    z_re = (num_re * lam_re + lb_im * lam_im) / den
    z_im = (lb_im * lam_re - num_re * lam_im) / den
    are_o[...] = lb_re
    aim_o[...] = lb_im
    pm = SSM_STATE * SSM_GROUP
    rep = (lax.broadcasted_iota(jnp.int32, (SSM_STATE, pm), 0)
           == lax.broadcasted_iota(jnp.int32, (SSM_STATE, pm), 1) // SSM_GROUP)
    rep = jnp.where(rep, 1.0, 0.0).astype(F32)
    zr = _mm_f32_lhs(z_re, rep)
    zi = _mm_f32_lhs(z_im, rep)
    b_re = bre_ref[...]
    b_im = bim_ref[...]
    bbre_o[...] = zr * b_re - zi * b_im
    bbim_o[...] = zr * b_im + zi * b_re


def _s5_params(lambda_re, lambda_im, log_step, b_re, b_im):
    g, p, m = b_re.shape
    outs = pl.pallas_call(
        _s5_params_kernel,
        out_shape=(jax.ShapeDtypeStruct((g, p), F32), jax.ShapeDtypeStruct((g, p), F32),
                   jax.ShapeDtypeStruct((g, p * m), F32), jax.ShapeDtypeStruct((g, p * m), F32)),
        name="s5_params",
    )(lambda_re, lambda_im, log_step.reshape(g, 1), b_re.reshape(g, p * m), b_im.reshape(g, p * m))
    a_re, a_im, bb_re, bb_im = outs
    return a_re, a_im, bb_re.reshape(g, p, m), bb_im.reshape(g, p, m)


def _inproj_kernel(x_ref, g1_ref, win_ref, mu_ref, w0_ref, wup_ref, a0_ref, aup_ref, gup_ref,
                   kk_ref, ka_ref,
                   r_o, lw_o, k_o, v_o, kk_o, b_o, g_o, u_o, carry_ref):
    ti = pl.program_id(1)
    h = _rms(x_ref[0], g1_ref[...])
    proj = _mm(h, win_ref[...])
    u_o[0] = proj[:, N_SHIFT:]
    p = proj[:, :N_SHIFT]
    tt = p.shape[0]
    carry = jnp.where(ti == 0, 0.0, carry_ref[0:1, :])
    row = lax.broadcasted_iota(jnp.int32, p.shape, 0)
    prev = jnp.where(row == 0, carry, pltpu.roll(p, 1, 0))
    carry_ref[0:1, :] = p[tt - 1:tt, :]
    ps = p + mu_ref[...] * (prev - p)

    r = ps[:, 0:D_RWKV]
    k = ps[:, D_RWKV:2 * D_RWKV]
    v = ps[:, 2 * D_RWKV:3 * D_RWKV]
    lora = ps[:, 3 * D_RWKV:3 * D_RWKV + LORA_W + LORA_A]
    g_lr = ps[:, 3 * D_RWKV + LORA_W + LORA_A:]

    z = w0_ref[...] + _mm(jnp.tanh(lora), wup_ref[...])
    nz = -z
    softplus = jnp.maximum(nz, 0.0) + jnp.log(1.0 + jnp.exp(-jnp.abs(nz)))
    log_w = -softplus - 0.5
    lw = -jnp.exp(log_w)
    a = _sigmoid(a0_ref[...] + _mm(lora, aup_ref[...]))
    g = _mm(_sigmoid(g_lr), gup_ref[...])

    kk = k * kk_ref[...]
    kk = kk / jnp.maximum(jnp.sqrt(_head_sum(kk * kk, 1.0)), 1e-12)
    k2 = k * (1.0 + (a - 1.0) * ka_ref[...])

    r_o[0] = r.astype(r_o.dtype)
    lw_o[0] = lw
    k_o[0] = k2.astype(k_o.dtype)
    v_o[0] = v.astype(v_o.dtype)
    kk_o[0] = kk.astype(kk_o.dtype)
    b_o[0] = (kk * a).astype(b_o.dtype)
    g_o[0] = g.astype(g_o.dtype)


def _inproj(x, norm1_g, w_in, mu_shift, w0, w_up, a0, a_up, g_up, k_k, k_a, tt):
    bsz, t_len, d = x.shape
    zeros_w = jnp.zeros((LORA_A, D_RWKV), F32)
    zeros_a = jnp.zeros((LORA_W, D_RWKV), F32)
    wup_pad = jnp.concatenate([w_up, zeros_w], axis=0).astype(_MXU)
    aup_pad = jnp.concatenate([zeros_a, a_up], axis=0).astype(_MXU)
    row = lambda v: v.reshape(1, -1)
    full = lambda shape: pl.BlockSpec(shape, lambda b, t: (0,) * len(shape))
    tok = lambda w: pl.BlockSpec((1, tt, w), lambda b, t: (b, t, 0))
    sds = lambda dt: jax.ShapeDtypeStruct((bsz, t_len, D_RWKV), dt)
    out_dtypes = [_MXU, F32, _MXU, _MXU, _MXU, _MXU, _MXU, F32]
    return pl.pallas_call(
        _inproj_kernel,
        grid=(bsz, t_len // tt),
        in_specs=[tok(d), full((1, d)), full((d, D_IN)), full((1, N_SHIFT)), full((1, D_RWKV)),
                  full((LORA_W + LORA_A, D_RWKV)), full((1, D_RWKV)), full((LORA_W + LORA_A, D_RWKV)),
                  full((LORA_G, D_RWKV)), full((1, D_RWKV)), full((1, D_RWKV))],
        out_specs=[tok(D_RWKV)] * 8,
        out_shape=[sds(dt) for dt in out_dtypes],
        scratch_shapes=[pltpu.VMEM((8, N_SHIFT), F32)],
        compiler_params=pltpu.CompilerParams(
            dimension_semantics=("arbitrary", "arbitrary"), vmem_limit_bytes=VMEM_LIMIT),
        name="inproj",
    )(x, row(norm1_g), w_in.astype(_MXU), row(mu_shift), row(w0), wup_pad, row(a0), aup_pad,
      g_up.astype(_MXU), row(k_k), row(k_a))


def _blockdiag(m, head0):
    return jnp.concatenate([jnp.where(head0, m, 0.0), jnp.where(head0, 0.0, m)], axis=0)


def _rwkv_kernel(r_ref, lw_ref, k_ref, v_ref, kk_ref, b_ref, g_ref, rk_ref, lnw_ref, lnb_ref,
                 y_o, s0_ref, s1_ref, s2_ref, s3_ref, y_scr,
                 w1_s, rt_s, arb_s, bh_s, kh_s, w2_s, ypar_s, gl_s):
    tb = pl.program_id(1)
    tc = r_ref.shape[1]
    n_chunks = tc // CHUNK

    @pl.when(tb == 0)
    def _():
        for s_ref in (s0_ref, s1_ref, s2_ref, s3_ref):
            s_ref[...] = jnp.zeros_like(s_ref)

    row = lax.broadcasted_iota(jnp.int32, (CHUNK, PAIR), 0)
    lane = lax.broadcasted_iota(jnp.int32, (CHUNK, PAIR), 1)
    head0 = lane < HEAD
    src = lane % HEAD
    strict = src < row
    incl = src <= row
    eye_pair = jnp.where(src == row, 1.0, 0.0).astype(F32)
    row2 = lax.broadcasted_iota(jnp.int32, (PAIR, PAIR), 0)
    lane2 = lax.broadcasted_iota(jnp.int32, (PAIR, PAIR), 1)
    same_head = (row2 < HEAD) == (lane2 < HEAD)
    tri = (lax.broadcasted_iota(jnp.int32, (CHUNK, CHUNK), 1)
           <= lax.broadcasted_iota(jnp.int32, (CHUNK, CHUNK), 0))
    tri = jnp.where(tri, 1.0, 0.0).astype(F32)

    pairs = range(N_PAIRS)
    psl = [slice(p * PAIR, (p + 1) * PAIR) for p in pairs]

    group = PREP_GROUP if n_chunks % PREP_GROUP == 0 else 1

    def prep_body(cg, carry):
        rows, a_t, r_t, b_t, k_t, v = [], [], [], [], [], []
        for j in range(group):
            c = cg * group + j
            rw = pl.ds(pl.multiple_of(c * CHUNK, CHUNK), CHUNK)
            lw = lw_ref[0, rw, :]
            cs = _mm_f32_rhs(tri, lw)
            cs_last = cs[CHUNK - 1:CHUNK, :]
            e_neg = jnp.exp(-cs)
            e_end = jnp.exp(cs_last - cs)
            kf = k_ref[0, rw, :].astype(F32)
            bf = b_ref[0, rw, :].astype(F32)
            rows.append(rw)
            r_t.append(r_ref[0, rw, :].astype(F32) * jnp.exp(cs))
            a_t.append(-kk_ref[0, rw, :].astype(F32) * jnp.exp(cs - lw))
            b_t.append(bf * e_neg)
            k_t.append(kf * e_neg)
            v.append(v_ref[0, rw, :].astype(F32))
            rt_s[rw, :] = r_t[j].astype(rt_s.dtype)
            bh_s[rw, :] = (bf * e_end).astype(bh_s.dtype)
            kh_s[rw, :] = (kf * e_end).astype(kh_s.dtype)
            gl_s[pl.ds(pl.multiple_of(c * 8, 8), 8), :] = jnp.broadcast_to(jnp.exp(cs_last), (8, D_RWKV))

        units = [(j, p) for j in range(group) for p in pairs]
        gram = [_mm_nt(jnp.concatenate([a_t[j][:, psl[p]], r_t[j][:, psl[p]]], axis=0),
                       jnp.concatenate([_blockdiag(b_t[j][:, psl[p]], head0),
                                        _blockdiag(k_t[j][:, psl[p]], head0)], axis=0)) for j, p in units]
        a_ab = [jnp.where(strict, gm[:CHUNK, :PAIR], 0.0) for gm in gram]
        a_ak = [jnp.where(strict, gm[:CHUNK, PAIR:], 0.0) for gm in gram]
        a_rb = [jnp.where(incl, gm[CHUNK:, :PAIR], 0.0) for gm in gram]
        a_rk = [jnp.where(incl, gm[CHUNK:, PAIR:], 0.0) for gm in gram]
        for i, (j, p) in enumerate(units):
            arb_s[rows[j], psl[p]] = a_rb[i].astype(arb_s.dtype)
        av = [_mm(jnp.concatenate([a_ak[i], a_rk[i]], axis=0), _blockdiag(v[j][:, psl[p]], head0))
              for i, (j, p) in enumerate(units)]
        for i, (j, p) in enumerate(units):
            ypar_s[rows[j], psl[p]] = av[i][CHUNK:]

        t_inv = [eye_pair + m for m in a_ab]
        m_pow = [_mm(m, _blockdiag(m, head0)) for m in a_ab]
        n_levels = CHUNK.bit_length() - 2
        for level in range(n_levels):
            if level + 1 < n_levels:
                out = [_mm(m, jnp.concatenate([_blockdiag(t, head0), _blockdiag(m, head0)], axis=1))
                       for t, m in zip(t_inv, m_pow)]
                t_inv = [t + o[:, :PAIR] for t, o in zip(t_inv, out)]
                m_pow = [o[:, PAIR:] for o in out]
            else:
                t_inv = [t + _mm(m, _blockdiag(t, head0)) for t, m in zip(t_inv, m_pow)]
        w12 = [_mm(t_inv[i], jnp.concatenate(
            [_blockdiag(a_t[j][:, psl[p]], head0), _blockdiag(av[i][:CHUNK], head0)], axis=1))
            for i, (j, p) in enumerate(units)]
        for i, (j, p) in enumerate(units):
            w1_s[rows[j], psl[p]] = w12[i][:, :PAIR].astype(w1_s.dtype)
            w2_s[rows[j], psl[p]] = w12[i][:, PAIR:]
        return carry

    lax.fori_loop(0, n_chunks // group, prep_body, 0)

    s_refs = (s0_ref, s1_ref, s2_ref, s3_ref)

    def state_body(c, carry):
        rows = pl.ds(pl.multiple_of(c * CHUNK, CHUNK), CHUNK)
        g_rows = pl.ds(pl.multiple_of(c * 8, 8), 8)
        s0 = [s_refs[p][...] for p in pairs]
        uy = [_mm_nt(jnp.concatenate([w1_s[rows, psl[p]], rt_s[rows, psl[p]]], axis=0), s0[p])
              for p in pairs]
        u = [uy[p][:CHUNK] + w2_s[rows, psl[p]] for p in pairs]
        y = [uy[p][CHUNK:] + ypar_s[rows, psl[p]] + _mm(arb_s[rows, psl[p]], _blockdiag(u[p], head0))
             for p in pairs]
        upd = [_mm_tn(jnp.concatenate([u[p], v_ref[0, rows, psl[p]]], axis=0),
                      jnp.concatenate([bh_s[rows, psl[p]], kh_s[rows, psl[p]]], axis=0)) for p in pairs]
        for p in pairs:
            y_scr[rows, psl[p]] = y[p]
            decay = jnp.concatenate([gl_s[g_rows, psl[p]]] * (PAIR // 8), axis=0)
            s_refs[p][...] = s0[p] * decay + jnp.where(same_head, upd[p], 0.0)
        return carry

    lax.fori_loop(0, n_chunks, state_body, 0)

    y = y_scr[...]
    mu = _head_sum(y, 1.0 / HEAD)
    d = y - mu
    var = _head_sum(d * d, 1.0 / HEAD)
    yn = d * lax.rsqrt(var + LN_X_EPS) * lnw_ref[...] + lnb_ref[...]
    rk = r_ref[0].astype(F32) * k_ref[0].astype(F32) * rk_ref[...]
    bonus = _head_sum(rk, 1.0) * v_ref[0].astype(F32)
    y_o[0] = ((yn + bonus) * g_ref[0].astype(F32)).astype(y_o.dtype)


def _rwkv(r, lw, k, v, kk, b, g, r_k, ln_w, ln_b, tc):
    bsz, t_len, _ = r.shape
    tok = pl.BlockSpec((1, tc, D_RWKV), lambda bb, t: (bb, t, 0))
    par = pl.BlockSpec((1, D_RWKV), lambda bb, t: (0, 0))
    return pl.pallas_call(
        _rwkv_kernel,
        grid=(bsz, t_len // tc),
        in_specs=[tok] * 7 + [par] * 3,
        out_specs=tok,
        out_shape=jax.ShapeDtypeStruct((bsz, t_len, D_RWKV), _MXU),
        scratch_shapes=([pltpu.VMEM((PAIR, PAIR), F32)] * N_PAIRS + [pltpu.VMEM((tc, D_RWKV), F32)]
                        + [pltpu.VMEM((tc, D_RWKV), _MXU)] * 5 + [pltpu.VMEM((tc, D_RWKV), F32)] * 2
                        + [pltpu.VMEM((tc // CHUNK * 8, D_RWKV), F32)]),
        compiler_params=pltpu.CompilerParams(
            dimension_semantics=("arbitrary", "arbitrary"), vmem_limit_bytes=VMEM_LIMIT),
        name="rwkv",
    )(r, lw, k, v, kk, b, g, r_k.reshape(1, D_RWKV), ln_w.reshape(1, D_RWKV), ln_b.reshape(1, D_RWKV))


def _s5_kernel(u_ref, unext_ref, are_ref, aim_ref, bbd_ref, cbd_ref, dskip_ref, wglu_ref, bglu_ref,
               beta_ref, y_o, sre_ref, sim_ref, bu0_ref, bu1_ref):
    i = pl.program_id(0)
    lc, bsz, _ = u_ref.shape

    def input_map(src_ref, dst_ref):
        ub = src_ref[...].reshape(lc * bsz, D_SSM).astype(_MXU)
        chan_per_tile = MXU_TILE // SSM_STATE * SSM_GROUP
        for j in range(2 * N_STATE // MXU_TILE):
            c0 = (j * MXU_TILE % N_STATE) // MXU_TILE * chan_per_tile // MXU_TILE * MXU_TILE
            cols = slice(j * MXU_TILE, (j + 1) * MXU_TILE)
            dst_ref[:, cols] = jnp.dot(ub[:, c0:c0 + MXU_TILE], bbd_ref[c0:c0 + MXU_TILE, cols],
                                       preferred_element_type=F32)

    @pl.when(i == 0)
    def _():
        sre_ref[...] = jnp.zeros_like(sre_ref)
        sim_ref[...] = jnp.zeros_like(sim_ref)
        input_map(u_ref, bu0_ref)

    def block_body(bu_ref, next_ref):
        input_map(unext_ref, next_ref)
        a_re = jnp.broadcast_to(are_ref[...], (bsz, N_STATE))
        a_im = jnp.broadcast_to(aim_ref[...], (bsz, N_STATE))

        def step(t, carry):
            s_re, s_im = carry
            rows = pl.ds(t * bsz, bsz)
            n_re = a_re * s_re - a_im * s_im + bu_ref[rows, :N_STATE]
            n_im = a_re * s_im + a_im * s_re + bu_ref[rows, N_STATE:]
            bu_ref[rows, :N_STATE] = n_re
            bu_ref[rows, N_STATE:] = n_im
            return n_re, n_im

        s_re, s_im = lax.fori_loop(0, lc, step, (sre_ref[...], sim_ref[...]), unroll=True)
        sre_ref[...] = s_re
        sim_ref[...] = s_im

        state_per_tile = MXU_TILE // SSM_GROUP * SSM_STATE
        parts = []
        for n in range(D_SSM // MXU_TILE):
            cols = slice(n * MXU_TILE, (n + 1) * MXU_TILE)
            re_rows = slice(n * state_per_tile, (n + 1) * state_per_tile)
            im_rows = slice(N_STATE + n * state_per_tile, N_STATE + (n + 1) * state_per_tile)
            parts.append(_mm(bu_ref[:, re_rows], cbd_ref[re_rows, cols])
                         + _mm(bu_ref[:, im_rows], cbd_ref[im_rows, cols]))
        y = jnp.concatenate(parts, axis=1) + dskip_ref[...] * u_ref[...].reshape(lc * bsz, D_SSM)
        y = jax.nn.gelu(y)
        y = y * _sigmoid(_mm(y, wglu_ref[...]) + bglu_ref[...])
        y_o[...] = _rms(y, beta_ref[...]).reshape(lc, bsz, D_SSM)

    @pl.when(i % 2 == 0)
    def _():
        block_body(bu0_ref, bu1_ref)

    @pl.when(i % 2 == 1)
    def _():
        block_body(bu1_ref, bu0_ref)


def _s5(u_tm, a_re, a_im, bb_re, bb_im, c_re, c_im, d_skip, w_glu, b_glu, beta, lc):
    t_len, bsz, _ = u_tm.shape
    eye = jnp.eye(N_GROUPS, dtype=F32)
    bbd = jnp.concatenate(
        [jnp.einsum('gpm,gh->gmhp', bb_re, eye).reshape(D_SSM, N_STATE),
         jnp.einsum('gpm,gh->gmhp', bb_im, eye).reshape(D_SSM, N_STATE)], axis=1).astype(_MXU)
    cbd = jnp.concatenate(
        [jnp.einsum('gmp,gh->gphm', c_re, eye).reshape(N_STATE, D_SSM),
         jnp.einsum('gmp,gh->gphm', -c_im, eye).reshape(N_STATE, D_SSM)], axis=0).astype(_MXU)
    full = lambda shape: pl.BlockSpec(shape, lambda i: (0,) * len(shape))
    n_steps = t_len // lc
    tok = pl.BlockSpec((lc, bsz, D_SSM), lambda i: (i, 0, 0))
    tok_next = pl.BlockSpec((lc, bsz, D_SSM), lambda i: (jnp.minimum(i + 1, n_steps - 1), 0, 0))
    return pl.pallas_call(
        _s5_kernel,
        grid=(n_steps,),
        in_specs=[tok, tok_next, full((1, N_STATE)), full((1, N_STATE)), full((D_SSM, 2 * N_STATE)),
                  full((2 * N_STATE, D_SSM)), full((1, D_SSM)), full((D_SSM, D_SSM)),
                  full((1, D_SSM)), full((1, D_SSM))],
        out_specs=tok,
        out_shape=jax.ShapeDtypeStruct((t_len, bsz, D_SSM), F32),
        scratch_shapes=[pltpu.VMEM((bsz, N_STATE), F32), pltpu.VMEM((bsz, N_STATE), F32),
                        pltpu.VMEM((lc * bsz, 2 * N_STATE), F32), pltpu.VMEM((lc * bsz, 2 * N_STATE), F32)],
        compiler_params=pltpu.CompilerParams(
            dimension_semantics=("arbitrary",), vmem_limit_bytes=VMEM_LIMIT),
        name="s5",
    )(u_tm, u_tm, a_re.reshape(1, N_STATE), a_im.reshape(1, N_STATE), bbd, cbd, d_skip.reshape(1, D_SSM),
      w_glu.astype(_MXU), b_glu.reshape(1, D_SSM), beta.reshape(1, D_SSM))


ROUTE_IDX = 0
ROUTE_COL = TOP_K
ROUTE_GATE = 2 * TOP_K
ROW_ALIGN = 8
BLK_BASE, BLK_CNT, BLK_OFF = 0, 1, 2


def _outproj_kernel(x_ref, yr_ref, ys_ref, wout_ref, g2_ref, rw_ref, rb_ref,
                    x2_o, h2_o, route_o, blk_o, tot_o, carry_ref):
    first = jnp.logical_and(pl.program_id(0) == 0, pl.program_id(1) == 0)

    @pl.when(first)
    def _():
        carry_ref[...] = jnp.zeros_like(carry_ref)

    x2 = (x_ref[0] + _mm(yr_ref[0], wout_ref[:D_RWKV, :]) + _mm(ys_ref[0], wout_ref[D_RWKV:, :]))
    x2_o[0] = x2
    h2 = _rms(x2, g2_ref[...])
    h2_o[0] = h2.astype(h2_o.dtype)
    logits = _mm_hi(h2, rw_ref[...]) + rb_ref[...]
    tt = logits.shape[0]
    lane = lax.broadcasted_iota(jnp.int32, (tt, LANES), 1).astype(F32)

    vals, idxs = [], []
    member = jnp.zeros((tt, LANES), F32)
    for _ in range(TOP_K):
        m = jnp.max(logits, axis=-1, keepdims=True)
        idx = jnp.min(jnp.where(logits == m, lane, float(LANES)), axis=-1, keepdims=True)
        sel = lane == idx
        member = jnp.where(sel, 1.0, member)
        logits = jnp.where(sel, -jnp.inf, logits)
        vals.append(m)
        idxs.append(idx)
    exps = [jnp.exp(v - vals[0]) for v in vals]
    denom = exps[0] + exps[1] + exps[2] + exps[3]

    before = (lax.broadcasted_iota(jnp.int32, (tt, tt), 1)
              < lax.broadcasted_iota(jnp.int32, (tt, tt), 0))
    rank_local = _mm(jnp.where(before, 1.0, 0.0), member)
    cnt = jnp.sum(member, axis=0, keepdims=True)
    cnt_al = jnp.floor((cnt + (ROW_ALIGN - 1.0)) * (1.0 / ROW_ALIGN)) * ROW_ALIGN
    upper = (lax.broadcasted_iota(jnp.int32, (LANES, LANES), 0)
             < lax.broadcasted_iota(jnp.int32, (LANES, LANES), 1))
    run_off = _mm_f32_lhs(jnp.broadcast_to(cnt_al, (8, LANES)), jnp.where(upper, 1.0, 0.0))[0:1, :]
    col_all = rank_local + run_off
    base = carry_ref[...]
    carry_ref[...] = base + cnt_al
    tot_o[...] = jnp.broadcast_to(base + cnt_al, tot_o.shape)
    rows8 = lax.broadcasted_iota(jnp.int32, (8, LANES), 0)
    blk_o[0] = jnp.where(rows8 == BLK_BASE, base,
                         jnp.where(rows8 == BLK_CNT, cnt_al, jnp.where(rows8 == BLK_OFF, run_off, 0.0)))

    route = jnp.zeros((tt, LANES), F32)
    for k in range(TOP_K):
        col_k = jnp.sum(jnp.where(lane == idxs[k], col_all, 0.0), axis=-1, keepdims=True)
        route = jnp.where(lane == ROUTE_IDX + k, idxs[k], route)
        route = jnp.where(lane == ROUTE_COL + k, col_k, route)
        route = jnp.where(lane == ROUTE_GATE + k, exps[k] / denom, route)
    route_o[0] = route


def _outproj(x, y_rwkv, y_ssm, w_out, norm2_g, router_w, router_b, tt):
    bsz, t_len, d = x.shape
    nt = t_len // tt
    rw = jnp.zeros((d, LANES), F32).at[:, :N_EXPERTS].set(router_w)
    rb = jnp.full((1, LANES), -1e30, F32).at[0, :N_EXPERTS].set(router_b)
    full = lambda shape: pl.BlockSpec(shape, lambda b, t: (0,) * len(shape))
    tok = lambda w: pl.BlockSpec((1, tt, w), lambda b, t: (b, t, 0))
    return pl.pallas_call(
        _outproj_kernel,
        grid=(bsz, nt),
        in_specs=[tok(d), tok(D_RWKV), tok(D_SSM), full((d, d)), full((1, d)), full((d, LANES)),
                  full((1, LANES))],
        out_specs=[tok(d), tok(d), tok(LANES), pl.BlockSpec((1, 8, LANES), lambda b, t: (b * nt + t, 0, 0)),
                   full((8, LANES))],
        out_shape=[jax.ShapeDtypeStruct((bsz, t_len, d), F32), jax.ShapeDtypeStruct((bsz, t_len, d), _MXU),
                   jax.ShapeDtypeStruct((bsz, t_len, LANES), F32),
                   jax.ShapeDtypeStruct((bsz * nt, 8, LANES), F32), jax.ShapeDtypeStruct((8, LANES), F32)],
        scratch_shapes=[pltpu.VMEM((1, LANES), F32)],
        compiler_params=pltpu.CompilerParams(
            dimension_semantics=("arbitrary", "arbitrary"), vmem_limit_bytes=VMEM_LIMIT),
        name="outproj_router",
    )(x, y_rwkv, y_ssm, w_out.astype(_MXU), norm2_g.reshape(1, d), rw, rb)


def _zero_tile_copy(z_ref, xs_ref, sem, start):
    return pltpu.make_async_copy(z_ref, xs_ref.at[pl.ds(start, FFN_TILE)], sem)


def _run_sizes(tt):
    sizes, s = [], ROW_ALIGN
    while s <= tt:
        sizes.append(s)
        s *= 2
    return sizes[::-1]


N_RUNS = N_EXPERTS + 1


def _for_each_run_piece(blk, tt, start_ref, cnt_ref, off_ref, fn):
    def per_expert(e, carry):
        j = blk * N_RUNS + e
        cnt, off, start = cnt_ref[j], off_ref[j], start_ref[j]
        done = 0
        for size in _run_sizes(tt):
            @pl.when((cnt & size) != 0)
            def _(done=done, size=size):
                fn(pl.multiple_of(off + done, ROW_ALIGN), pl.multiple_of(start + done, ROW_ALIGN), size)
            done = done + (cnt & size)
        return carry

    lax.fori_loop(0, N_RUNS, per_expert, 0)


def _dispatch_kernel(start_ref, cnt_ref, off_ref, ends_ref, padded_ref, route_ref, h_ref, xs_o,
                     x_buf, z_ref, sems, z_sem):
    i = pl.program_id(0)
    n_blk = pl.num_programs(0)
    tt = h_ref.shape[0]
    n_buf = x_buf.shape[1]
    slot = i % 2

    def start_pieces(buf_slot):
        def fn(buf_row, sorted_row, size):
            pltpu.make_async_copy(x_buf.at[buf_slot, pl.ds(buf_row, size)],
                                  xs_o.at[pl.ds(sorted_row, size)], sems.at[buf_slot]).start()
        return fn

    def wait_block(buf_slot):
        pltpu.make_async_copy(x_buf.at[buf_slot], xs_o.at[pl.ds(0, n_buf)], sems.at[buf_slot]).wait()

    @pl.when(i == 0)
    def _():
        z_ref[...] = jnp.zeros_like(z_ref)
        for e in range(N_EXPERTS):
            @pl.when(padded_ref[e] > 0)
            def _():
                start = pl.multiple_of(ends_ref[e] - FFN_TILE, FFN_TILE)
                _zero_tile_copy(z_ref, xs_o, z_sem, start).start()
        n_used = ends_ref[N_EXPERTS - 1] // FFN_TILE
        n_tiles = xs_o.shape[0] // FFN_TILE

        def zero_tail(j, carry):
            _zero_tile_copy(z_ref, xs_o, z_sem, pl.multiple_of(j * FFN_TILE, FFN_TILE)).start()
            return carry

        def wait_tail(j, carry):
            _zero_tile_copy(z_ref, xs_o, z_sem, 0).wait()
            return carry

        lax.fori_loop(n_used, n_tiles, zero_tail, 0)
        for e in range(N_EXPERTS):
            @pl.when(padded_ref[e] > 0)
            def _():
                _zero_tile_copy(z_ref, xs_o, z_sem, 0).wait()
        lax.fori_loop(n_used, n_tiles, wait_tail, 0)

    route_t = route_ref[...].T
    rows = lax.broadcasted_iota(jnp.int32, (n_buf, tt), 0).astype(F32)
    onehot = jnp.zeros((n_buf, tt), F32)
    for k in range(TOP_K):
        onehot = jnp.where(rows == route_t[ROUTE_COL + k:ROUTE_COL + k + 1, :], 1.0, onehot)
    x_buf[slot] = _mm(onehot, h_ref[...])

    @pl.when(i > 0)
    def _():
        wait_block(1 - slot)

    _for_each_run_piece(i, tt, start_ref, cnt_ref, off_ref, start_pieces(slot))

    @pl.when(i == n_blk - 1)
    def _():
        wait_block(slot)


def _block_buffer_rows(tt):
    return TOP_K * tt + N_EXPERTS * ROW_ALIGN


def _dispatch(start, cnt, off, ends, padded, route, h2, n_rows, tt):
    n, d = h2.shape
    return pl.pallas_call(
        _dispatch_kernel,
        grid_spec=pltpu.PrefetchScalarGridSpec(
            num_scalar_prefetch=5,
            grid=(n // tt,),
            in_specs=[pl.BlockSpec((tt, LANES), lambda i, *_: (i, 0)),
                      pl.BlockSpec((tt, d), lambda i, *_: (i, 0))],
            out_specs=pl.BlockSpec(memory_space=pl.ANY),
            scratch_shapes=[pltpu.VMEM((2, _block_buffer_rows(tt), d), F32), pltpu.VMEM((FFN_TILE, d), F32),
                            pltpu.SemaphoreType.DMA((2,)), pltpu.SemaphoreType.DMA]),
        out_shape=jax.ShapeDtypeStruct((n_rows, d), F32),
        compiler_params=pltpu.CompilerParams(
            dimension_semantics=("arbitrary",), vmem_limit_bytes=VMEM_LIMIT),
        name="moe_dispatch",
    )(start, cnt, off, ends, padded, route, h2)


def _w1_perm_kernel(w_ref, o_ref):
    r = lax.broadcasted_iota(jnp.int32, (MXU_TILE, MXU_TILE), 0)
    c = lax.broadcasted_iota(jnp.int32, (MXU_TILE, MXU_TILE), 1)
    half = MXU_TILE // 2
    src = jnp.where(c < half, 2 * c, 2 * (c - half) + 1)
    perm = jnp.where(r == src, 1.0, 0.0).astype(_MXU)
    for j in range(w_ref.shape[2] // MXU_TILE):
        sl = slice(j * MXU_TILE, (j + 1) * MXU_TILE)
        o_ref[0, :, sl] = jnp.dot(w_ref[0, :, sl].astype(_MXU), perm,
                                  preferred_element_type=F32).astype(o_ref.dtype)


def _w1_perm(w1):
    e, d, f2 = w1.shape
    spec = pl.BlockSpec((1, d, f2), lambda i: (i, 0, 0))
    return pl.pallas_call(
        _w1_perm_kernel,
        grid=(e,),
        in_specs=[spec],
        out_specs=spec,
        out_shape=jax.ShapeDtypeStruct(w1.shape, _MXU),
        compiler_params=pltpu.CompilerParams(
            dimension_semantics=("arbitrary",), vmem_limit_bytes=VMEM_LIMIT),
        name="w1_regroup",
    )(w1)


def _ffn_kernel(te_ref, nused_ref, xs_ref, w1_ref, b1g_ref, b1l_ref, w2_ref, b2_ref, ys_o):
    i = pl.program_id(0)

    @pl.when(i < nused_ref[0])
    def _():
        hu = _mm(xs_ref[...], w1_ref[0])
        half = MXU_TILE // 2
        n_grp = hu.shape[1] // MXU_TILE
        h_glu = jnp.concatenate([hu[:, j * MXU_TILE:j * MXU_TILE + half] for j in range(n_grp)], axis=1)
        h_lin = jnp.concatenate([hu[:, j * MXU_TILE + half:(j + 1) * MXU_TILE] for j in range(n_grp)], axis=1)
        x_glu = jnp.minimum(h_glu + b1g_ref[0], SWIGLU_LIMIT)
        x_lin = jnp.clip(h_lin + b1l_ref[0], -SWIGLU_LIMIT, SWIGLU_LIMIT)
        act = x_glu * _sigmoid(SWIGLU_ALPHA * x_glu) * (x_lin + 1.0)
        ys_o[...] = _mm(act, w2_ref[0]) + b2_ref[0]

    @pl.when(i >= nused_ref[0])
    def _():
        ys_o[...] = jnp.zeros_like(ys_o)


def _ffn(tile_expert, n_used, xs, w1p, b1g, b1l, w2, b2):
    n_rows, d = xs.shape
    n_tiles = n_rows // FFN_TILE

    def row_map(i, te, nu):
        return (jnp.minimum(i, nu[0] - 1), 0)

    def exp_map(i, te, nu):
        return (te[i], 0, 0)

    return pl.pallas_call(
        _ffn_kernel,
        grid_spec=pltpu.PrefetchScalarGridSpec(
            num_scalar_prefetch=2,
            grid=(n_tiles,),
            in_specs=[pl.BlockSpec((FFN_TILE, d), row_map),
                      pl.BlockSpec((1, d, 2 * D_FF), exp_map),
                      pl.BlockSpec((1, 1, D_FF), exp_map), pl.BlockSpec((1, 1, D_FF), exp_map),
                      pl.BlockSpec((1, D_FF, d), exp_map), pl.BlockSpec((1, 1, d), exp_map)],
            out_specs=pl.BlockSpec((FFN_TILE, d), lambda i, te, nu: (i, 0))),
        out_shape=jax.ShapeDtypeStruct((n_rows, d), F32),
        compiler_params=pltpu.CompilerParams(
            dimension_semantics=("arbitrary",), vmem_limit_bytes=VMEM_LIMIT),
        name="moe_ffn",
    )(tile_expert, n_used, xs, w1p, b1g, b1l, w2, b2)


def _combine_kernel(start_ref, cnt_ref, off_ref, route_ref, x2_ref, ys_hbm, fg_ref, out_o, y_buf, sems):
    i = pl.program_id(0)
    n_blk = pl.num_programs(0)
    tt = x2_ref.shape[0]
    n_buf = y_buf.shape[1]
    slot = i % 2

    def start_pieces(buf_slot):
        def fn(buf_row, sorted_row, size):
            pltpu.make_async_copy(ys_hbm.at[pl.ds(sorted_row, size)],
                                  y_buf.at[buf_slot, pl.ds(buf_row, size)], sems.at[buf_slot]).start()
        return fn

    @pl.when(i == 0)
    def _():
        _for_each_run_piece(0, tt, start_ref, cnt_ref, off_ref, start_pieces(0))

    @pl.when(i + 1 < n_blk)
    def _():
        _for_each_run_piece(i + 1, tt, start_ref, cnt_ref, off_ref, start_pieces(1 - slot))

    pltpu.make_async_copy(ys_hbm.at[pl.ds(0, n_buf)], y_buf.at[slot], sems.at[slot]).wait()

    route = route_ref[...]
    cols = lax.broadcasted_iota(jnp.int32, (tt, n_buf), 1).astype(F32)
    gates = jnp.zeros((tt, n_buf), F32)
    for k in range(TOP_K):
        gates = jnp.where(cols == route[:, ROUTE_COL + k:ROUTE_COL + k + 1],
                          route[:, ROUTE_GATE + k:ROUTE_GATE + k + 1], gates)
    acc = x2_ref[...] + _mm(gates, y_buf[slot])
    out_o[...] = _rms(acc, fg_ref[...])


def _combine(start, cnt, off, route, x2, ys, final_g, tt):
    n, d = x2.shape
    return pl.pallas_call(
        _combine_kernel,
        grid_spec=pltpu.PrefetchScalarGridSpec(
            num_scalar_prefetch=3,
            grid=(n // tt,),
            in_specs=[pl.BlockSpec((tt, LANES), lambda i, *_: (i, 0)),
                      pl.BlockSpec((tt, d), lambda i, *_: (i, 0)),
                      pl.BlockSpec(memory_space=pl.ANY), pl.BlockSpec((1, d), lambda i, *_: (0, 0))],
            out_specs=pl.BlockSpec((tt, d), lambda i, *_: (i, 0)),
            scratch_shapes=[pltpu.VMEM((2, _block_buffer_rows(tt), d), F32), pltpu.SemaphoreType.DMA((2,))]),
        out_shape=jax.ShapeDtypeStruct((n, d), F32),
        compiler_params=pltpu.CompilerParams(
            dimension_semantics=("arbitrary",), vmem_limit_bytes=VMEM_LIMIT),
        name="moe_combine",
    )(start, cnt, off, route, x2, ys, final_g.reshape(1, d))


def _moe(x2, h2, route, blk, totals, w1, b1, w2, b2, final_g, tt):
    n, d = x2.shape
    n_blk = n // tt
    tot = totals.astype(jnp.int32)
    padded = (tot + FFN_TILE - 1) // FFN_TILE * FFN_TILE
    ends = jnp.cumsum(padded)
    offs = ends - padded
    blk = blk[:, :, :N_EXPERTS].astype(jnp.int32)
    n_buf = _block_buffer_rows(tt)
    n_rows = -(-(n * TOP_K + n_blk * N_EXPERTS * ROW_ALIGN + N_EXPERTS * FFN_TILE + n_buf)
               // FFN_TILE) * FFN_TILE
    used = jnp.sum(blk[:, BLK_CNT], axis=1, keepdims=True)
    filler = n_buf - used
    col = lambda a, extra: jnp.concatenate([a, extra], axis=1).reshape(-1)
    cnt = col(blk[:, BLK_CNT], filler)
    off = col(blk[:, BLK_OFF], used)
    start = offs[None, :] + blk[:, BLK_BASE]
    start_out = col(start, jnp.full_like(used, n_rows - n_buf))
    start_in = col(start, jnp.zeros_like(used))
    n_tiles = n_rows // FFN_TILE
    tile_start = jnp.arange(n_tiles, dtype=jnp.int32) * FFN_TILE
    tile_expert = jnp.minimum(jnp.sum((tile_start[:, None] >= ends[None, :]).astype(jnp.int32), axis=1),
                              N_EXPERTS - 1)
    n_used = ends[-1:] // FFN_TILE

    xs = _dispatch(start_out, cnt, off, ends, padded, route, h2, n_rows, tt)
    ys = _ffn(tile_expert, n_used, xs, _w1_perm(w1), b1[:, None, 0::2], b1[:, None, 1::2],
              w2.astype(_MXU), b2[:, None, :])
    return _combine(start_in, cnt, off, route, x2, ys, final_g, tt)


def _tile(t_len, want):
    return want if t_len % want == 0 else t_len


def kernel(x, norm1_g, w_in, mu_shift, w0, w_up, a0, a_up, g_up, k_k, k_a, r_k, ln_x_w, ln_x_b,
           lambda_re, lambda_im, log_step, b_re, b_im, c_re, c_im, d_skip, w_glu, b_glu, beta_ssm,
           w_out, norm2_g, router_w, router_b, w1, b1, w2, b2, final_g):
    bsz, t_len, d = x.shape
    assert d == D_MODEL and t_len % CHUNK == 0 and norm1_g.shape[0] == 1
    tt = _tile(t_len, 512)
    a_re, a_im, bb_re, bb_im = _s5_params(lambda_re[0], lambda_im[0], log_step[0], b_re[0], b_im[0])
    r, lw, k, v, kk, b, g, u = _inproj(x, norm1_g[0], w_in[0], mu_shift[0], w0[0], w_up[0], a0[0],
                                       a_up[0], g_up[0], k_k[0], k_a[0], tt)
    y_rwkv = _rwkv(r, lw, k, v, kk, b, g, r_k[0], ln_x_w[0], ln_x_b[0], _tile(t_len, 512))
    y_ssm = _s5(jnp.swapaxes(u, 0, 1), a_re, a_im, bb_re, bb_im, c_re[0], c_im[0], d_skip[0],
                w_glu[0], b_glu[0], beta_ssm[0], _tile(t_len, 64))
    y_ssm = jnp.swapaxes(y_ssm, 0, 1)
    tb = _tile(t_len, 512)
    x2, h2, route, blk, totals = _outproj(x, y_rwkv, y_ssm, w_out[0], norm2_g[0], router_w[0],
                                          router_b[0], tb)
    n = bsz * t_len
    out = _moe(x2.reshape(n, d), h2.reshape(n, d), route.reshape(n, LANES), blk, totals[0, :N_EXPERTS],
               w1[0], b1[0], w2[0], b2[0], final_g, tb)
    return out.reshape(bsz, t_len, d)
```
